```python
import jax
import jax.numpy as jnp
from jax import lax
import numpy as np

D_MODEL = 1024
BATCH = 16
SEQ = 2048
DEPTH = 4
DEC_BATCH = 32
DEC_SEQ = 64
PAST_LEN = 4096

CHUNK = 64
N_A_LAYERS = DEPTH // 2
N_B_LAYERS = DEPTH - N_A_LAYERS
N_DENSE = (DEPTH + 1) // 2
N_MOE = DEPTH // 2
GDN_HEADS = 8
GDN_DK = 128
GDN_DV = 128
GDN_QK_W = GDN_HEADS * GDN_DK
GDN_V_W = GDN_HEADS * GDN_DV
GDN_CONV_DIM = 2 * GDN_QK_W + GDN_V_W
GDN_IN_W = GDN_CONV_DIM + GDN_V_W + 2 * GDN_HEADS
CONV_W = 4
FOX_HEADS = 16
FOX_HEAD_DIM = D_MODEL // FOX_HEADS
FOX_W = FOX_HEADS * FOX_HEAD_DIM
KV_PROJ_W = 2 * FOX_W + FOX_HEADS
Q_BLOCK = 128
D_FF = 2816
N_EXPERTS = 8
TOP_K = 2
D_FF_EXPERT = 2816
NORM_EPS = 1e-6

kernel_name = "yoco_gdn_fox_stream_step"


def _rmsnorm(x, g):
    xf = x.astype(jnp.float32)
    y = xf * lax.rsqrt(jnp.mean(xf * xf, axis=-1, keepdims=True) + NORM_EPS)
    return (y * g.astype(jnp.float32)).astype(x.dtype)


def _l2norm(x):
    return x * lax.rsqrt(jnp.sum(x * x, axis=-1, keepdims=True) + 1e-6)


def _gated_delta_rule(q, k, v, g, beta, s0):
    B, T, H, _ = q.shape
    DV = v.shape[-1]
    C = min(CHUNK, T)
    N = T // C

    def blocks(a):
        return jnp.moveaxis(a.reshape((B, N, C, H) + a.shape[3:]), 3, 1)

    q, k, v, g, beta = blocks(q), blocks(k), blocks(v), blocks(g), blocks(beta)
    gc = jnp.cumsum(g, axis=-1)
    diff = gc[..., :, None] - gc[..., None, :]
    idx = jnp.arange(C)
    strict = idx[:, None] > idx[None, :]
    causal = idx[:, None] >= idx[None, :]
    kb = k * beta[..., None]
    lower = jnp.where(strict, jnp.einsum('bhncd,bhnsd->bhncs', kb, k) * jnp.exp(jnp.where(strict, diff, 0.0)), 0.0)
    m = lower + jnp.eye(C, dtype=jnp.float32)
    u = lax.linalg.triangular_solve(m, v * beta[..., None], left_side=True, lower=True, unit_diagonal=True)
    w = lax.linalg.triangular_solve(m, kb * jnp.exp(gc)[..., None], left_side=True, lower=True, unit_diagonal=True)
    a_qk = jnp.where(causal, jnp.einsum('bhncd,bhnsd->bhncs', q, k) * jnp.exp(jnp.where(causal, diff, 0.0)), 0.0)
    q_dec = q * jnp.exp(gc)[..., None]
    k_tail = k * jnp.exp(gc[..., -1:] - gc)[..., None]
    g_last = jnp.exp(gc[..., -1])

    def step(s, xs):
        q_c, k_c, u_c, w_c, a_c, gl = xs
        v_new = u_c - jnp.einsum('bhcd,bhde->bhce', w_c, s)
        o = jnp.einsum('bhcd,bhde->bhce', q_c, s) + jnp.einsum('bhcs,bhse->bhce', a_c, v_new)
        s = s * gl[..., None, None] + jnp.einsum('bhcd,bhce->bhde', k_c, v_new)
        return s, o

    xs = (jnp.moveaxis(q_dec, 2, 0), jnp.moveaxis(k_tail, 2, 0), jnp.moveaxis(u, 2, 0),
          jnp.moveaxis(w, 2, 0), jnp.moveaxis(a_qk, 2, 0), jnp.moveaxis(g_last, 2, 0))
    s_final, o = lax.scan(step, s0, xs)
    o = jnp.moveaxis(jnp.moveaxis(o, 0, 2), 1, 3).reshape(B, T, H, DV)
    return o, s_final


def _gdn_mixer(h, conv_ctx, s0, w_in, conv_w, a_log, dt_bias, o_gain, w_out):
    B, T, _ = h.shape
    proj = h @ w_in
    o1 = GDN_CONV_DIM
    o2 = o1 + GDN_V_W
    o3 = o2 + GDN_HEADS
    qkv, z, b_raw, a_raw = proj[..., :o1], proj[..., o1:o2], proj[..., o2:o3], proj[..., o3:]
    xp = jnp.concatenate([conv_ctx.astype(qkv.dtype), qkv], axis=1)
    conv = xp[:, 0:T] * conv_w[0]
    for i in range(1, CONV_W):
        conv = conv + xp[:, i:i + T] * conv_w[i]
    qkv_c = jax.nn.silu(conv).astype(jnp.float32)
    q = qkv_c[..., :GDN_QK_W].reshape(B, T, GDN_HEADS, GDN_DK)
    k = qkv_c[..., GDN_QK_W:2 * GDN_QK_W].reshape(B, T, GDN_HEADS, GDN_DK)
    v = qkv_c[..., 2 * GDN_QK_W:].reshape(B, T, GDN_HEADS, GDN_DV)
    q = _l2norm(q) * (GDN_DK ** -0.5)
    k = _l2norm(k)
    beta = jax.nn.sigmoid(b_raw.astype(jnp.float32))
    g = -jnp.exp(a_log.astype(jnp.float32)) * jax.nn.softplus(a_raw.astype(jnp.float32) + dt_bias.astype(jnp.float32))
    o, s = _gated_delta_rule(q, k, v, g, beta, s0.astype(jnp.float32))
    o = _rmsnorm(o, o_gain) * jax.nn.silu(z.astype(jnp.float32).reshape(B, T, GDN_HEADS, GDN_DV))
    y = o.reshape(B, T, GDN_V_W).astype(h.dtype) @ w_out
    return y, xp[:, -(CONV_W - 1):], s


def _fox_attention(q, k, v, f_q, f_k, q_off):
    B, Tq, H, hd = q.shape
    Tk = k.shape[1]
    blk = min(Q_BLOCK, Tq)
    nb = Tq // blk
    q_blocks = jnp.moveaxis(q.reshape(B, nb, blk, H, hd), 1, 0)
    fq_blocks = jnp.moveaxis(f_q.reshape(B, nb, blk, H), 1, 0)
    f_k_t = jnp.swapaxes(f_k, 1, 2)
    k_pos = jnp.arange(Tk)
    scale = hd ** -0.5

    def one_block(args):
        q_b, fq_b, start = args
        s = jnp.einsum('bqhd,bkhd->bhqk', q_b, k, preferred_element_type=jnp.float32) * scale
        s = s + (jnp.swapaxes(fq_b, 1, 2)[..., :, None] - f_k_t[:, :, None, :])
        q_pos = q_off + start + jnp.arange(blk)
        s = jnp.where(k_pos[None, :] <= q_pos[:, None], s, -jnp.inf)
        p = jax.nn.softmax(s, axis=-1)
        return jnp.einsum('bhqk,bkhd->bqhd', p.astype(v.dtype), v)

    o = lax.map(one_block, (q_blocks, fq_blocks, jnp.arange(nb) * blk))
    return jnp.moveaxis(o, 0, 1).reshape(B, Tq, H, hd)


def _swiglu(h, w_gate, w_up, w_down):
    return (jax.nn.silu(h @ w_gate) * (h @ w_up)) @ w_down


def _moe(h, router_w, router_b, w_gate, w_up, w_down):
    logits = (h @ router_w).astype(jnp.float32) + router_b.astype(jnp.float32)
    top_v, top_i = lax.top_k(logits, TOP_K)
    wts = jax.nn.softmax(top_v, axis=-1)
    gate = jnp.sum(jax.nn.one_hot(top_i, N_EXPERTS, dtype=jnp.float32) * wts[..., None], axis=-2)
    y = jnp.zeros(h.shape, h.dtype)
    for e in range(N_EXPERTS):
        y = y + gate[..., e:e + 1].astype(h.dtype) * _swiglu(h, w_gate[e], w_up[e], w_down[e])
    return y


def _trunk(x, c, conv_ctx, s0, past_k, past_v, past_logf, p):
    B, T, _ = x.shape
    sc = jax.nn.silu(c)
    conv_out, s_out = [], []
    k_new = v_new = logf_new = None
    k_all = v_all = f_q = f_k = None
    q_off = 0
    for l in range(DEPTH):
        mod = (sc @ p['ada_w'][l] + p['ada_b'][l])[:, None, :]
        sh_m, sc_m, gt_m, sh_f, sc_f, gt_f = jnp.split(mod, 6, axis=-1)
        h = _rmsnorm(x, p['pre_mix_g'][l]) * (1 + sc_m) + sh_m
        if l < N_A_LAYERS:
            y, ctx, s = _gdn_mixer(h, conv_ctx[:, l], s0[:, l], p['gdn_w_in'][l], p['gdn_conv_w'][l],
                                   p['gdn_a_log'][l], p['gdn_dt_bias'][l], p['gdn_o_gain'][l], p['gdn_w_out'][l])
            conv_out.append(ctx)
            s_out.append(s)
        else:
            j = l - N_A_LAYERS
            q = (h @ p['fox_w_q'][j]).reshape(B, T, FOX_HEADS, FOX_HEAD_DIM)
            o = _fox_attention(q, k_all, v_all, f_q, f_k, q_off)
            y = o.reshape(B, T, FOX_W) @ p['fox_w_o'][j]
        x = x + gt_m * _rmsnorm(y, p['post_mix_g'][l])
        h = _rmsnorm(x, p['pre_ffn_g'][l]) * (1 + sc_f) + sh_f
        if l % 2 == 0:
            i = l // 2
            y = _swiglu(h, p['ffn_w_gate'][i], p['ffn_w_up'][i], p['ffn_w_down'][i])
        else:
            i = l // 2
            y = _moe(h, p['moe_router_w'][i], p['moe_router_b'][i], p['moe_w_gate'][i],
                     p['moe_w_up'][i], p['moe_w_down'][i])
        x = x + gt_f * _rmsnorm(y, p['post_ffn_g'][l])
        if l == N_A_LAYERS - 1:
            kvm = (sc @ p['kv_ada_w'] + p['kv_ada_b'])[:, None, :]
            sh_kv, sc_kv = jnp.split(kvm, 2, axis=-1)
            hk = _rmsnorm(x, p['kv_norm_g']) * (1 + sc_kv) + sh_kv
            kvp = hk @ p['kv_w']
            k_new = kvp[..., :FOX_W].reshape(B, T, FOX_HEADS, FOX_HEAD_DIM)
            v_new = kvp[..., FOX_W:2 * FOX_W].reshape(B, T, FOX_HEADS, FOX_HEAD_DIM)
            logf_new = jax.nn.log_sigmoid(kvp[..., 2 * FOX_W:].astype(jnp.float32) + p['kv_b_f'].astype(jnp.float32))
            if past_k is None:
                k_all, v_all, logf_all, q_off = k_new, v_new, logf_new, 0
            else:
                k_all = jnp.concatenate([past_k.astype(k_new.dtype), k_new], axis=1)
                v_all = jnp.concatenate([past_v.astype(v_new.dtype), v_new], axis=1)
                logf_all = jnp.concatenate([past_logf.astype(jnp.float32), logf_new], axis=1)
                q_off = past_k.shape[1]
            f_k = jnp.cumsum(logf_all, axis=1)
            f_q = f_k[:, q_off:]
    return x, k_new, v_new, logf_new, jnp.stack(s_out, axis=1), jnp.stack(conv_out, axis=1)


def setup_inputs(seed: int = 0) -> dict:
    key = jax.random.key(seed)
    ks = list(jax.random.split(key, 48))
    cnt = [0]

    def nk():
        cnt[0] += 1
        return ks[cnt[0] - 1]

    def nrm(shape, scale):
        return jax.random.normal(nk(), shape, jnp.float32) * scale

    D = D_MODEL
    dt = jnp.exp(jax.random.uniform(nk(), (N_A_LAYERS, GDN_HEADS), jnp.float32, np.log(1e-3), np.log(1e-1)))
    return {
        'x_prompt': nrm((BATCH, SEQ, D), 1.0),
        'x_sample': nrm((DEC_BATCH, DEC_SEQ, D), 1.0),
        'c_prompt': nrm((BATCH, D), 1.0),
        'c_sample': nrm((DEC_BATCH, D), 1.0),
        'cache_k': nrm((DEC_BATCH, PAST_LEN, FOX_HEADS, FOX_HEAD_DIM), 1.0),
        'cache_v': nrm((DEC_BATCH, PAST_LEN, FOX_HEADS, FOX_HEAD_DIM), 1.0),
        'cache_logf': jax.nn.log_sigmoid(2.5 + nrm((DEC_BATCH, PAST_LEN, FOX_HEADS), 1.0)),
        'state_delta': nrm((DEC_BATCH, N_A_LAYERS, GDN_HEADS, GDN_DK, GDN_DV), 0.1),
        'state_conv': nrm((DEC_BATCH, N_A_LAYERS, CONV_W - 1, GDN_CONV_DIM), 1.0),
        'ada_w': nrm((DEPTH, D, 6 * D), 0.5 * D ** -0.5),
        'ada_b': nrm((DEPTH, 6 * D), 0.02),
        'pre_mix_g': 1.0 + nrm((DEPTH, D), 0.05),
        'post_mix_g': 1.0 + nrm((DEPTH, D), 0.05),
        'pre_ffn_g': 1.0 + nrm((DEPTH, D), 0.05),
        'post_ffn_g': 1.0 + nrm((DEPTH, D), 0.05),
        'gdn_w_in': nrm((N_A_LAYERS, D, GDN_IN_W), D ** -0.5),
        'gdn_conv_w': nrm((N_A_LAYERS, CONV_W, GDN_CONV_DIM), CONV_W ** -0.5),
        'gdn_a_log': jnp.log(jax.random.uniform(nk(), (N_A_LAYERS, GDN_HEADS), jnp.float32, 1.0, 16.0)),
        'gdn_dt_bias': dt + jnp.log(-jnp.expm1(-dt)),
        'gdn_o_gain': 1.0 + nrm((N_A_LAYERS, GDN_DV), 0.05),
        'gdn_w_out': nrm((N_A_LAYERS, GDN_V_W, D), GDN_V_W ** -0.5),
        'kv_norm_g': 1.0 + nrm((D,), 0.05),
        'kv_ada_w': nrm((D, 2 * D), 0.5 * D ** -0.5),
        'kv_ada_b': nrm((2 * D,), 0.02),
        'kv_w': nrm((D, KV_PROJ_W), D ** -0.5),
        'kv_b_f': jax.random.uniform(nk(), (FOX_HEADS,), jnp.float32, 1.0, 4.0),
        'fox_w_q': nrm((N_B_LAYERS, D, FOX_W), D ** -0.5),
        'fox_w_o': nrm((N_B_LAYERS, FOX_W, D), FOX_W ** -0.5),
        'ffn_w_gate': nrm((N_DENSE, D, D_FF), D ** -0.5),
        'ffn_w_up': nrm((N_DENSE, D, D_FF), D ** -0.5),
        'ffn_w_down': nrm((N_DENSE, D_FF, D), D_FF ** -0.5),
        'moe_router_w': nrm((N_MOE, D, N_EXPERTS), D ** -0.5),
        'moe_router_b': nrm((N_MOE, N_EXPERTS), 0.01),
        'moe_w_gate': nrm((N_MOE, N_EXPERTS, D, D_FF_EXPERT), D ** -0.5),
        'moe_w_up': nrm((N_MOE, N_EXPERTS, D, D_FF_EXPERT), D ** -0.5),
        'moe_w_down': nrm((N_MOE, N_EXPERTS, D_FF_EXPERT, D), D_FF_EXPERT ** -0.5),
    }


def reference(x_prompt, x_sample, c_prompt, c_sample, cache_k, cache_v, cache_logf, state_delta, state_conv,
              ada_w, ada_b, pre_mix_g, post_mix_g, pre_ffn_g, post_ffn_g,
              gdn_w_in, gdn_conv_w, gdn_a_log, gdn_dt_bias, gdn_o_gain, gdn_w_out,
              kv_norm_g, kv_ada_w, kv_ada_b, kv_w, kv_b_f, fox_w_q, fox_w_o,
              ffn_w_gate, ffn_w_up, ffn_w_down,
              moe_router_w, moe_router_b, moe_w_gate, moe_w_up, moe_w_down):
    p = {
        'ada_w': ada_w, 'ada_b': ada_b, 'pre_mix_g': pre_mix_g, 'post_mix_g': post_mix_g,
        'pre_ffn_g': pre_ffn_g, 'post_ffn_g': post_ffn_g,
        'gdn_w_in': gdn_w_in, 'gdn_conv_w': gdn_conv_w, 'gdn_a_log': gdn_a_log, 'gdn_dt_bias': gdn_dt_bias,
        'gdn_o_gain': gdn_o_gain, 'gdn_w_out': gdn_w_out,
        'kv_norm_g': kv_norm_g, 'kv_ada_w': kv_ada_w, 'kv_ada_b': kv_ada_b, 'kv_w': kv_w, 'kv_b_f': kv_b_f,
        'fox_w_q': fox_w_q, 'fox_w_o': fox_w_o,
        'ffn_w_gate': ffn_w_gate, 'ffn_w_up': ffn_w_up, 'ffn_w_down': ffn_w_down,
        'moe_router_w': moe_router_w, 'moe_router_b': moe_router_b, 'moe_w_gate': moe_w_gate,
        'moe_w_up': moe_w_up, 'moe_w_down': moe_w_down,
    }
    bp = x_prompt.shape[0]
    conv0 = jnp.zeros((bp, N_A_LAYERS, CONV_W - 1, GDN_CONV_DIM), x_prompt.dtype)
    s00 = jnp.zeros((bp, N_A_LAYERS, GDN_HEADS, GDN_DK, GDN_DV), jnp.float32)
    y_prompt, k_p, v_p, f_p, d_p, c_p = _trunk(x_prompt, c_prompt, conv0, s00, None, None, None, p)
    y_sample, k_s, v_s, f_s, d_s, c_s = _trunk(x_sample, c_sample, state_conv, state_delta,
                                              cache_k, cache_v, cache_logf, p)
    return (y_prompt, y_sample, k_p, v_p, f_p, d_p, c_p, k_s, v_s, f_s, d_s, c_s)
```

```python
import functools

import jax
import jax.numpy as jnp
import numpy as np
from jax import lax
from jax.experimental import pallas as pl
from jax.experimental.pallas import tpu as pltpu

F32 = jnp.float32
BF16 = jnp.bfloat16
NORM_EPS = 1e-6
L2_EPS = 1e-6
GROUP = 64
LANES = 128
SUBLANES = 8
VMEM_LIMIT_BYTES = 56 * 1024 * 1024
CONV_W = 4
TOP_K = 2
NEG_INF = float("-inf")


def _sigmoid(x):
    return 1.0 / (1.0 + jnp.exp(-x))


def _silu(x):
    return x * _sigmoid(x)


def _softplus(x):
    return jnp.maximum(x, 0.0) + jnp.log1p(jnp.exp(-jnp.abs(x)))


def _rms(x, g):
    return x * lax.rsqrt(jnp.mean(x * x, axis=-1, keepdims=True) + NORM_EPS) * g


def _bdot(a, b):
    return jnp.dot(a.astype(BF16), b.astype(BF16), preferred_element_type=F32)


def _bdot_nt(a, b):
    return lax.dot_general(a.astype(BF16), b.astype(BF16), (((1,), (1,)), ((), ())),
                           preferred_element_type=F32)


def _bdot_tn(a, b):
    return lax.dot_general(a.astype(BF16), b.astype(BF16), (((0,), (0,)), ((), ())),
                           preferred_element_type=F32)


def _split2(x):
    hi = x.astype(BF16)
    lo = (x - hi.astype(F32)).astype(BF16)
    return hi, lo


def _dot_split(a, b):
    a_hi, a_lo = _split2(a)
    b_hi, b_lo = _split2(b)
    dot = functools.partial(jnp.dot, preferred_element_type=F32)
    return dot(a_hi, b_hi) + (dot(a_hi, b_lo) + dot(a_lo, b_hi))


def _lane_col(x, idx):
    lane = lax.broadcasted_iota(jnp.int32, x.shape, 1)
    return jnp.sum(jnp.where(lane == idx, x, 0.0), axis=1, keepdims=True)


def _modulated_norm(x_ref, sh_ref, sc_ref, g_ref, groups, emit):
    g = g_ref[...]
    for gi in range(groups):
        rows = pl.ds(gi * GROUP, GROUP)
        h = _rms(x_ref[rows, :], g) * (1.0 + sc_ref[gi:gi + 1, :]) + sh_ref[gi:gi + 1, :]
        emit(rows, h)


def _gated_residual(y_ref, x_ref, gt_ref, g_ref, o_ref, groups):
    g = g_ref[...]
    for gi in range(groups):
        rows = pl.ds(gi * GROUP, GROUP)
        o_ref[rows, :] = x_ref[rows, :] + gt_ref[gi:gi + 1, :] * _rms(y_ref[rows, :], g)


def _tile(n, cap, unit):
    best = None
    for t in range(unit, min(n, cap) + 1, unit):
        if n % t == 0:
            best = t
    assert best is not None, (n, cap, unit)
    return best


def _params(semantics):
    return pltpu.CompilerParams(dimension_semantics=semantics, vmem_limit_bytes=VMEM_LIMIT_BYTES)


def _mod_kernel(c_ref, w_ref, b_ref, o_ref):
    s = _silu(c_ref[...])
    o_ref[...] = _bdot(s, w_ref[...]) + b_ref[...]


def _mod_matmul(c, w, b, tn=1024):
    n_l, d, width = w.shape
    m = c.shape[0]
    tn = _tile(width, tn, LANES)
    return pl.pallas_call(
        _mod_kernel,
        grid=(n_l, width // tn),
        in_specs=[pl.BlockSpec((m, d), lambda l, j: (0, 0)),
                  pl.BlockSpec((None, d, tn), lambda l, j: (l, 0, j)),
                  pl.BlockSpec((None, 1, tn), lambda l, j: (l, 0, j))],
        out_specs=pl.BlockSpec((None, m, tn), lambda l, j: (l, 0, j)),
        out_shape=jax.ShapeDtypeStruct((n_l, m, width), F32),
        compiler_params=_params(("arbitrary", "arbitrary")),
        name="ada_mod",
    )(c, w, b.reshape(n_l, 1, width))


def _norm_mm_kernel(x_ref, sh_ref, sc_ref, g_ref, w_ref, *rest, groups, log_sigmoid):
    if log_sigmoid:
        b_ref, o_ref, h_ref = rest
    else:
        o_ref, h_ref = rest

    @pl.when(pl.program_id(1) == 0)
    def _():
        def emit(rows, h):
            h_ref[rows, :] = h.astype(BF16)
        _modulated_norm(x_ref, sh_ref, sc_ref, g_ref, groups, emit)

    y = jnp.dot(h_ref[...], w_ref[...], preferred_element_type=F32)
    if log_sigmoid:
        y = -_softplus(-(y + b_ref[...]))
    o_ref[...] = y.astype(o_ref.dtype)


def _norm_matmul(x, modg, sh_col, sc_col, g, w, *, out_dtype, bias=None, tm=1024, tn=2048, name):
    n, d = x.shape
    width = w.shape[1]
    tm = _tile(n, tm, GROUP * SUBLANES)
    tn = _tile(width, tn, LANES)
    groups = tm // GROUP
    in_specs = [pl.BlockSpec((tm, d), lambda i, j: (i, 0)),
                pl.BlockSpec((groups, d), lambda i, j: (i, sh_col)),
                pl.BlockSpec((groups, d), lambda i, j: (i, sc_col)),
                pl.BlockSpec((1, d), lambda i, j: (0, 0)),
                pl.BlockSpec((d, tn), lambda i, j: (0, j))]
    args = [x, modg, modg, g.reshape(1, d), w]
    if bias is not None:
        in_specs.append(pl.BlockSpec((1, tn), lambda i, j: (0, j)))
        args.append(bias.reshape(1, width))
    return pl.pallas_call(
        functools.partial(_norm_mm_kernel, groups=groups, log_sigmoid=bias is not None),
        grid=(n // tm, width // tn),
        in_specs=in_specs,
        out_specs=pl.BlockSpec((tm, tn), lambda i, j: (i, j)),
        out_shape=jax.ShapeDtypeStruct((n, width), out_dtype),
        scratch_shapes=[pltpu.VMEM((tm, d), BF16)],
        compiler_params=_params(("arbitrary", "arbitrary")),
        name=name,
    )(*args)


def _mm_post_kernel(a_ref, w_ref, x_ref, gt_ref, g_ref, o_ref, y_ref, *, groups):
    y_ref[...] = jnp.dot(a_ref[...], w_ref[...], preferred_element_type=F32)
    _gated_residual(y_ref, x_ref, gt_ref, g_ref, o_ref, groups)


def _matmul_post(a, w, x, modg, gt_col, g, *, tm=512, name):
    n, d = x.shape
    k = a.shape[1]
    tm = _tile(n, tm, GROUP * SUBLANES)
    groups = tm // GROUP
    return pl.pallas_call(
        functools.partial(_mm_post_kernel, groups=groups),
        grid=(n // tm,),
        in_specs=[pl.BlockSpec((tm, k), lambda i: (i, 0)),
                  pl.BlockSpec((k, d), lambda i: (0, 0)),
                  pl.BlockSpec((tm, d), lambda i: (i, 0)),
                  pl.BlockSpec((groups, d), lambda i: (i, gt_col)),
                  pl.BlockSpec((1, d), lambda i: (0, 0))],
        out_specs=pl.BlockSpec((tm, d), lambda i: (i, 0)),
        out_shape=jax.ShapeDtypeStruct((n, d), F32),
        scratch_shapes=[pltpu.VMEM((tm, d), F32)],
        compiler_params=_params(("arbitrary",)),
        name=name,
    )(a, w, x, modg, g.reshape(1, d))


def _top2_gates(logits, n_experts):
    lane = lax.broadcasted_iota(jnp.int32, logits.shape, 1).astype(F32)
    big = float(LANES)
    logits = jnp.where(lane < n_experts, logits, NEG_INF)
    m1 = jnp.max(logits, axis=1, keepdims=True)
    i1 = jnp.min(jnp.where(logits == m1, lane, big), axis=1, keepdims=True)
    rest = jnp.where(lane == i1, NEG_INF, logits)
    m2 = jnp.max(rest, axis=1, keepdims=True)
    i2 = jnp.min(jnp.where(rest == m2, lane, big), axis=1, keepdims=True)
    e2 = jnp.exp(m2 - m1)
    den = 1.0 + e2
    return jnp.where(lane == i1, 1.0 / den, 0.0) + jnp.where(lane == i2, e2 / den, 0.0)


def _ffn_kernel(x_ref, sh_ref, sc_ref, gt_ref, gpre_ref, gpost_ref, *rest, groups, n_experts):
    if n_experts > 1:
        rw_hi_ref, rw_lo_ref, rb_ref, wg_ref, wu_ref, wd_ref, o_ref, h_ref, acc_ref, gates_ref = rest
    else:
        wg_ref, wu_ref, wd_ref, o_ref, h_ref, acc_ref = rest
    e = pl.program_id(1)
    j = pl.program_id(2)

    @pl.when((e == 0) & (j == 0))
    def _():
        def emit(rows, h):
            h_ref[rows, :] = h.astype(BF16)
            if n_experts > 1:
                h_hi, h_lo = _split2(h)
                dot = functools.partial(jnp.dot, preferred_element_type=F32)
                logits = dot(h_hi, rw_hi_ref[...]) + (dot(h_hi, rw_lo_ref[...]) + dot(h_lo, rw_hi_ref[...]))
                gates_ref[rows, :] = _top2_gates(logits + rb_ref[...], n_experts)
        _modulated_norm(x_ref, sh_ref, sc_ref, gpre_ref, groups, emit)
        acc_ref[...] = jnp.zeros_like(acc_ref)

    h = h_ref[...]
    gate = jnp.dot(h, wg_ref[...], preferred_element_type=F32)
    up = jnp.dot(h, wu_ref[...], preferred_element_type=F32)
    act = (_silu(gate) * up).astype(BF16)
    y = jnp.dot(act, wd_ref[...], preferred_element_type=F32)
    if n_experts > 1:
        y = y * _lane_col(gates_ref[...], e)
    acc_ref[...] += y

    @pl.when((e == n_experts - 1) & (j == pl.num_programs(2) - 1))
    def _():
        _gated_residual(acc_ref, x_ref, gt_ref, gpost_ref, o_ref, groups)


def _ffn(x, modg, g_pre, g_post, wg, wu, wd, router=None, *, tm=512, tf=1408, name):
    n, d = x.shape
    n_experts, _, f = wg.shape
    tm = _tile(n, tm, GROUP * SUBLANES)
    tf = _tile(f, tf, LANES)
    groups = tm // GROUP
    row = lambda i, e, j: (i, 0)
    in_specs = [pl.BlockSpec((tm, d), row),
                pl.BlockSpec((groups, d), lambda i, e, j: (i, 3)),
                pl.BlockSpec((groups, d), lambda i, e, j: (i, 4)),
                pl.BlockSpec((groups, d), lambda i, e, j: (i, 5)),
                pl.BlockSpec((1, d), lambda i, e, j: (0, 0)),
                pl.BlockSpec((1, d), lambda i, e, j: (0, 0))]
    args = [x, modg, modg, modg, g_pre.reshape(1, d), g_post.reshape(1, d)]
    scratch = [pltpu.VMEM((tm, d), BF16), pltpu.VMEM((tm, d), F32)]
    if n_experts > 1:
        rw_hi, rw_lo, rb = router
        in_specs += [pl.BlockSpec((d, LANES), lambda i, e, j: (0, 0)),
                     pl.BlockSpec((d, LANES), lambda i, e, j: (0, 0)),
                     pl.BlockSpec((1, LANES), lambda i, e, j: (0, 0))]
        args += [rw_hi, rw_lo, rb]
        scratch.append(pltpu.VMEM((tm, LANES), F32))
    in_specs += [pl.BlockSpec((None, d, tf), lambda i, e, j: (e, 0, j)),
                 pl.BlockSpec((None, d, tf), lambda i, e, j: (e, 0, j)),
                 pl.BlockSpec((None, tf, d), lambda i, e, j: (e, j, 0))]
    args += [wg, wu, wd]
    return pl.pallas_call(
        functools.partial(_ffn_kernel, groups=groups, n_experts=n_experts),
        grid=(n // tm, n_experts, f // tf),
        in_specs=in_specs,
        out_specs=pl.BlockSpec((tm, d), row),
        out_shape=jax.ShapeDtypeStruct((n, d), F32),
        scratch_shapes=scratch,
        compiler_params=_params(("arbitrary", "arbitrary", "arbitrary")),
        name=name,
    )(*args)


def _unit_lower_inverse(low):
    c = low.shape[0]
    n_blk = c // SUBLANES
    row = lax.broadcasted_iota(jnp.int32, (SUBLANES, c), 0)
    col = lax.broadcasted_iota(jnp.int32, (SUBLANES, c), 1)
    low_blk = [low[r * SUBLANES:(r + 1) * SUBLANES, :] for r in range(n_blk)]
    inv_blk = [jnp.where(col == row + r * SUBLANES, 1.0, 0.0) for r in range(n_blk)]
    for j in range(c - 1):
        rj = j // SUBLANES
        pivot_row = inv_blk[rj][j % SUBLANES:j % SUBLANES + 1, :]
        for r in range(rj, n_blk):
            inv_blk[r] = inv_blk[r] - low_blk[r][:, j:j + 1] * pivot_row
    return jnp.concatenate(inv_blk, axis=0)


def _gdn_kernel(seq_ref, first_ref, last_ref,
                proj_ref, ba_ref, ctx_ref, cw_ref, alog_ref, dtb_ref, gain_ref, s0_ref,
                og_ref, sout_ref, xs_ref, st_ref, *, heads, dk, dv):
    del seq_ref
    i = pl.program_id(0)
    c = GROUP
    pad = SUBLANES
    qk_w = heads * dk
    conv_dim = 2 * qk_w + heads * dv

    @pl.when(first_ref[i] == 1)
    def _():
        xs_ref[0:pad, :] = ctx_ref[...]
        st_ref[...] = s0_ref[...]

    @pl.when(first_ref[i] == 0)
    def _():
        xs_ref[0:pad, :] = xs_ref[c:c + pad, :]

    xs_ref[pad:pad + c, :] = proj_ref[:, 0:conv_dim]

    row = lax.broadcasted_iota(jnp.int32, (c, c), 0)
    col = lax.broadcasted_iota(jnp.int32, (c, c), 1)
    strict = row > col
    causal = row >= col
    eye = row == col
    ba = ba_ref[...]
    gain = gain_ref[...]

    def conv(c0):
        acc = None
        for t in range(CONV_W):
            start = pad - (CONV_W - 1) + t
            term = xs_ref[start:start + c, c0:c0 + LANES] * cw_ref[t:t + 1, c0:c0 + LANES]
            acc = term if acc is None else acc + term
        return _silu(acc)

    for hh in range(heads):
        q = conv(hh * dk)
        k = conv(qk_w + hh * dk)
        v = conv(2 * qk_w + hh * dv)
        q = q * lax.rsqrt(jnp.sum(q * q, axis=-1, keepdims=True) + L2_EPS) * (dk ** -0.5)
        k = k * lax.rsqrt(jnp.sum(k * k, axis=-1, keepdims=True) + L2_EPS)
        beta = _sigmoid(_lane_col(ba, hh))
        g = -jnp.exp(_lane_col(alog_ref[...], hh)) * _softplus(_lane_col(ba, heads + hh)
                                                                + _lane_col(dtb_ref[...], hh))
        g_row = jnp.sum(jnp.where(eye, g, 0.0), axis=0, keepdims=True)
        gc_col = jnp.sum(jnp.where(causal, g_row, 0.0), axis=1, keepdims=True)
        gc_row = jnp.sum(jnp.where(row <= col, g, 0.0), axis=0, keepdims=True)
        diff = gc_col - gc_row
        kb = k * beta
        lower = jnp.where(strict, _bdot_nt(kb, k) * jnp.exp(jnp.where(strict, diff, 0.0)), 0.0)
        a_qk = jnp.where(causal, _bdot_nt(q, k) * jnp.exp(jnp.where(causal, diff, 0.0)), 0.0)
        inv = _unit_lower_inverse(lower)
        eg = jnp.exp(gc_col)
        uw = _dot_split(inv, jnp.concatenate([v * beta, kb * eg], axis=1))
        u = uw[:, :dv]
        w = uw[:, dv:]
        gc_last = gc_col[c - 1:c, :]
        q_dec = q * eg
        k_tail = k * jnp.exp(gc_last - gc_col)
        s = st_ref[hh]
        ws_qs = _bdot(jnp.concatenate([w, q_dec], axis=0), s)
        v_new = u - ws_qs[:c]
        o = ws_qs[c:] + _bdot(a_qk, v_new)
        st_ref[hh] = s * jnp.exp(gc_last) + _bdot_tn(k_tail, v_new)
        z = proj_ref[:, conv_dim + hh * dv:conv_dim + (hh + 1) * dv]
        og_ref[:, hh * dv:(hh + 1) * dv] = (_rms(o, gain) * _silu(z)).astype(og_ref.dtype)

    @pl.when(last_ref[i] == 1)
    def _():
        sout_ref[...] = st_ref[...]


def _gdn_core(proj, ba, ctx8, conv_w, a_log, dt_bias, o_gain, s0, seq_of_chunk, first, last, *, heads, dk, dv):
    n, pw = proj.shape
    conv_dim = 2 * heads * dk + heads * dv
    n_seq = s0.shape[0]
    c = GROUP
    assert n % c == 0 and dk == LANES and dv == LANES

    def lane_row(vec):
        return jnp.zeros((1, LANES), F32).at[0, :vec.shape[0]].set(vec.astype(F32))

    grid_spec = pltpu.PrefetchScalarGridSpec(
        num_scalar_prefetch=3,
        grid=(n // c,),
        in_specs=[pl.BlockSpec((c, pw), lambda i, s, f, l: (i, 0)),
                  pl.BlockSpec((c, LANES), lambda i, s, f, l: (i, 0)),
                  pl.BlockSpec((None, SUBLANES, conv_dim), lambda i, s, f, l: (s[i], 0, 0)),
                  pl.BlockSpec((CONV_W, conv_dim), lambda i, s, f, l: (0, 0)),
                  pl.BlockSpec((1, LANES), lambda i, s, f, l: (0, 0)),
                  pl.BlockSpec((1, LANES), lambda i, s, f, l: (0, 0)),
                  pl.BlockSpec((1, dv), lambda i, s, f, l: (0, 0)),
                  pl.BlockSpec((None, heads, dk, dv), lambda i, s, f, l: (s[i], 0, 0, 0))],
        out_specs=[pl.BlockSpec((c, heads * dv), lambda i, s, f, l: (i, 0)),
                   pl.BlockSpec((None, heads, dk, dv), lambda i, s, f, l: (s[i], 0, 0, 0))],
        scratch_shapes=[pltpu.VMEM((c + SUBLANES, conv_dim), F32),
                        pltpu.VMEM((heads, dk, dv), F32)],
    )
    return pl.pallas_call(
        functools.partial(_gdn_kernel, heads=heads, dk=dk, dv=dv),
        grid_spec=grid_spec,
        out_shape=[jax.ShapeDtypeStruct((n, heads * dv), BF16),
                   jax.ShapeDtypeStruct((n_seq, heads, dk, dv), F32)],
        compiler_params=_params(("arbitrary",)),
        name="gdn_core",
    )(seq_of_chunk, first, last, proj, ba, ctx8, conv_w, lane_row(a_log), lane_row(dt_bias),
      o_gain.reshape(1, dv).astype(F32), s0)


def _cumsum_kernel(x_ref, o_ref):
    heads, t = x_ref.shape
    upper = (lax.broadcasted_iota(jnp.int32, (LANES, LANES), 0)
             <= lax.broadcasted_iota(jnp.int32, (LANES, LANES), 1)).astype(BF16)

    def body(i, carry):
        cols = pl.ds(pl.multiple_of(i * LANES, LANES), LANES)
        x = x_ref[:, cols]
        x1 = x.astype(BF16)
        r1 = x - x1.astype(F32)
        x2 = r1.astype(BF16)
        x3 = (r1 - x2.astype(F32)).astype(BF16)
        dot = functools.partial(jnp.dot, preferred_element_type=F32)
        out = carry + (dot(x1, upper) + (dot(x2, upper) + dot(x3, upper)))
        o_ref[:, cols] = out
        return out[:, LANES - 1:LANES]

    lax.fori_loop(0, t // LANES, body, jnp.zeros((heads, 1), F32))


def _cumsum_rows(x):
    b, heads, t = x.shape
    assert t % LANES == 0
    return pl.pallas_call(
        _cumsum_kernel,
        grid=(b,),
        in_specs=[pl.BlockSpec((None, heads, t), lambda i: (i, 0, 0))],
        out_specs=pl.BlockSpec((None, heads, t), lambda i: (i, 0, 0)),
        out_shape=jax.ShapeDtypeStruct((b, heads, t), F32),
        compiler_params=_params(("arbitrary",)),
        name="logf_cumsum",
    )(x)


def _fox_kernel(q_ref, kn_ref, vn_ref, *rest, tq, tc, n_cache, hd):
    if n_cache:
        kc_ref, vc_ref, f_ref, o_ref, kb_ref, vb_ref = rest
    else:
        f_ref, o_ref, kb_ref, vb_ref = rest
    qi = pl.program_id(2)
    q_off = n_cache * tc

    @pl.when(qi == 0)
    def _():
        kb_ref[...] = kn_ref[...].astype(BF16)
        vb_ref[...] = vn_ref[...].astype(BF16)

    lane = lax.broadcasted_iota(jnp.int32, (tq, LANES), 1)
    row = lax.broadcasted_iota(jnp.int32, (tq, tq), 0)
    col = lax.broadcasted_iota(jnp.int32, (tq, tq), 1)
    q2 = q_ref[...] * (hd ** -0.5)
    single_q_tile = kn_ref.shape[0] == tq
    q_start = 0 if single_q_tile else pl.multiple_of(qi * tq, tq)
    outs = []
    for half in range(LANES // hd):
        in_half = (lane >= half * hd) & (lane < (half + 1) * hd)
        qh = jnp.where(in_half, q2, jnp.zeros_like(q2))
        f_q_row = f_ref[half, :, pl.ds(q_off + q_start, tq)]
        f_q = jnp.sum(jnp.where(row == col, f_q_row, 0.0), axis=1, keepdims=True)

        def tile(carry, k_t, v_t, f_k_row, mask):
            m, l, acc = carry
            s = lax.dot_general(qh, k_t, (((1,), (1,)), ((), ())), preferred_element_type=F32)
            s = s + (f_q - f_k_row)
            if mask is not None:
                s = jnp.where(mask, s, NEG_INF)
            m_new = jnp.maximum(m, jnp.max(s, axis=1, keepdims=True))
            alpha = jnp.exp(m - m_new)
            p = jnp.exp(s - m_new)
            l = alpha * l + jnp.sum(p, axis=1, keepdims=True)
            acc = alpha * acc + jnp.dot(p.astype(BF16), v_t, preferred_element_type=F32)
            return m_new, l, acc

        carry = (jnp.full((tq, 1), NEG_INF, F32), jnp.zeros((tq, 1), F32), jnp.zeros((tq, LANES), F32))

        if n_cache:
            def cache_body(t, carry):
                rows = pl.ds(pl.multiple_of(t * tc, tc), tc)
                return tile(carry, kc_ref[rows, :].astype(BF16), vc_ref[rows, :].astype(BF16),
                            f_ref[half, :, rows], None)
            carry = lax.fori_loop(0, n_cache, cache_body, carry)

        def new_body(t, carry):
            start = pl.multiple_of(t * tq, tq)
            rows = pl.ds(start, tq)
            return tile(carry, kb_ref[rows, :], vb_ref[rows, :], f_ref[half, :, pl.ds(q_off + start, tq)], None)
        if not single_q_tile:
            carry = lax.fori_loop(0, qi, new_body, carry)

        rows = pl.ds(q_start, tq)
        _, l, acc = tile(carry, kb_ref[rows, :], vb_ref[rows, :], f_q_row, row >= col)
        outs.append((in_half, acc / l))

    out = outs[0][1]
    for in_half, val in outs[1:]:
        out = jnp.where(in_half, val, out)
    o_ref[...] = out.astype(o_ref.dtype)


def _fox_attention(q, k_new, v_new, f_rows, row0, n_batch, t_new, cache=None, *, heads, hd, tq, tc=512):
    width = heads * hd
    per_blk = LANES // hd
    assert t_new % tq == 0 and row0 % t_new == 0 and row0 % tq == 0
    assert tq % LANES == 0 or tq == t_new
    n_q = t_new // tq
    n_cache = 0
    in_specs = [pl.BlockSpec((tq, LANES), lambda b, h, i: (row0 // tq + b * n_q + i, h)),
                pl.BlockSpec((t_new, LANES), lambda b, h, i: (row0 // t_new + b, h)),
                pl.BlockSpec((t_new, LANES), lambda b, h, i: (row0 // t_new + b, h))]
    args = [q, k_new, v_new]
    if cache is not None:
        t_cache = cache[0].shape[1]
        tc = _tile(t_cache, tc, LANES)
        n_cache = t_cache // tc
        in_specs += [pl.BlockSpec((None, t_cache, LANES), lambda b, h, i: (b, 0, h)),
                     pl.BlockSpec((None, t_cache, LANES), lambda b, h, i: (b, 0, h))]
        args += list(cache)
    t_pad = f_rows.shape[-1]
    in_specs.append(pl.BlockSpec((None, per_blk, 1, t_pad), lambda b, h, i: (b, h, 0, 0)))
    args.append(f_rows)
    return pl.pallas_call(
        functools.partial(_fox_kernel, tq=tq, tc=tc, n_cache=n_cache, hd=hd),
        grid=(n_batch, width // LANES, n_q),
        in_specs=in_specs,
        out_specs=pl.BlockSpec((tq, LANES), lambda b, h, i: (b * n_q + i, h)),
        out_shape=jax.ShapeDtypeStruct((n_batch * t_new, width), BF16),
        scratch_shapes=[pltpu.VMEM((t_new, LANES), BF16), pltpu.VMEM((t_new, LANES), BF16)],
        compiler_params=_params(("arbitrary", "arbitrary", "arbitrary")),
        name="fox_cache" if cache is not None else "fox_prompt",
    )(*args)


def kernel(x_prompt, x_sample, c_prompt, c_sample, cache_k, cache_v, cache_logf, state_delta, state_conv, ada_w, ada_b, pre_mix_g, post_mix_g, pre_ffn_g, post_ffn_g, gdn_w_in, gdn_conv_w, gdn_a_log, gdn_dt_bias, gdn_o_gain, gdn_w_out, kv_norm_g, kv_ada_w, kv_ada_b, kv_w, kv_b_f, fox_w_q, fox_w_o, ffn_w_gate, ffn_w_up, ffn_w_down, moe_router_w, moe_router_b, moe_w_gate, moe_w_up, moe_w_down):
    bp, tp, d = x_prompt.shape
    bs, ts, _ = x_sample.shape
    n_p, n_s = bp * tp, bs * ts
    n = n_p + n_s
    depth = ada_w.shape[0]
    n_gdn = gdn_w_in.shape[0]
    gdn_heads, gdn_dk, gdn_dv = state_delta.shape[2:]
    qk_w, v_w = gdn_heads * gdn_dk, gdn_heads * gdn_dv
    conv_dim = 2 * qk_w + v_w
    fox_heads, fox_hd = cache_k.shape[2:]
    fox_w = fox_heads * fox_hd
    t_cache = cache_k.shape[1]
    n_experts = moe_router_w.shape[-1]
    assert tp % GROUP == 0 and ts % GROUP == 0

    x = jnp.concatenate([x_prompt.reshape(n_p, d), x_sample.reshape(n_s, d)], axis=0)
    c = jnp.concatenate([c_prompt, c_sample], axis=0)

    group_batch = np.concatenate([np.repeat(np.arange(bp), tp // GROUP),
                                  bp + np.repeat(np.arange(bs), ts // GROUP)])

    def per_group(m):
        n_l, _, width = m.shape
        rep = lambda a, b, t: jnp.broadcast_to(a[:, :, None, :], (n_l, b, t // GROUP, width)).reshape(n_l, -1, width)
        return jnp.concatenate([rep(m[:, :bp], bp, tp), rep(m[:, bp:], bs, ts)], axis=1)

    modg = per_group(_mod_matmul(c, ada_w, ada_b))
    kv_modg = per_group(_mod_matmul(c, kv_ada_w[None], kv_ada_b[None]))[0]

    seq_of_chunk = jnp.asarray(group_batch, jnp.int32)
    chunk_in_seq = np.concatenate([np.tile(np.arange(tp // GROUP), bp), np.tile(np.arange(ts // GROUP), bs)])
    seq_chunks = np.concatenate([np.full(n_p // GROUP, tp // GROUP), np.full(n_s // GROUP, ts // GROUP)])
    first = jnp.asarray(chunk_in_seq == 0, jnp.int32)
    last = jnp.asarray(chunk_in_seq == seq_chunks - 1, jnp.int32)

    delta_out, conv_out = [], []
    k_all = v_all = logf = f_p = f_s = None
    for l in range(depth):
        g_mix = pre_mix_g[l]
        if l < n_gdn:
            w_in = gdn_w_in[l]
            w_qkvz = w_in[:, :conv_dim + v_w].astype(BF16)
            w_ba = jnp.zeros((d, LANES), BF16).at[:, :2 * gdn_heads].set(w_in[:, conv_dim + v_w:].astype(BF16))
            proj = _norm_matmul(x, modg[l], 0, 1, g_mix, w_qkvz, out_dtype=F32, name="gdn_in_proj")
            ba = _norm_matmul(x, modg[l], 0, 1, g_mix, w_ba, out_dtype=F32, name="gdn_gate_proj")
            ctx = jnp.concatenate([jnp.zeros((bp, CONV_W - 1, conv_dim), F32), state_conv[:, l]], axis=0)
            ctx8 = jnp.pad(ctx, ((0, 0), (SUBLANES - (CONV_W - 1), 0), (0, 0)))
            s0 = jnp.concatenate([jnp.zeros((bp,) + state_delta.shape[2:], F32), state_delta[:, l]], axis=0)
            mixed, s_out = _gdn_core(proj, ba, ctx8, gdn_conv_w[l], gdn_a_log[l], gdn_dt_bias[l], gdn_o_gain[l],
                                     s0, seq_of_chunk, first, last, heads=gdn_heads, dk=gdn_dk, dv=gdn_dv)
            delta_out.append(s_out)
            tail = lambda a, b, t: a.reshape(b, t, -1)[:, t - (CONV_W - 1):, :conv_dim]
            conv_out.append(jnp.concatenate([tail(proj[:n_p], bp, tp), tail(proj[n_p:], bs, ts)], axis=0))
            w_o = gdn_w_out[l].astype(BF16)
        else:
            j = l - n_gdn
            q = _norm_matmul(x, modg[l], 0, 1, g_mix, fox_w_q[j].astype(BF16), out_dtype=BF16, name="fox_q_proj")
            o_p = _fox_attention(q, k_all, v_all, f_p, 0, bp, tp, heads=fox_heads, hd=fox_hd, tq=min(tp, 256))
            o_s = _fox_attention(q, k_all, v_all, f_s, n_p, bs, ts,
                                 cache=(cache_k.reshape(bs, t_cache, fox_w), cache_v.reshape(bs, t_cache, fox_w)),
                                 heads=fox_heads, hd=fox_hd, tq=ts)
            mixed = jnp.concatenate([o_p, o_s], axis=0)
            w_o = fox_w_o[j].astype(BF16)
        x = _matmul_post(mixed, w_o, x, modg[l], 2, post_mix_g[l], name="mixer_out_proj")

        i = l // 2
        if l % 2 == 0:
            x = _ffn(x, modg[l], pre_ffn_g[l], post_ffn_g[l], ffn_w_gate[i][None].astype(BF16),
                     ffn_w_up[i][None].astype(BF16), ffn_w_down[i][None].astype(BF16), name="dense_ffn")
        else:
            rw = jnp.zeros((d, LANES), F32).at[:, :n_experts].set(moe_router_w[i])
            rw_hi = rw.astype(BF16)
            rw_lo = (rw - rw_hi.astype(F32)).astype(BF16)
            rb = jnp.zeros((1, LANES), F32).at[0, :n_experts].set(moe_router_b[i])
            x = _ffn(x, modg[l], pre_ffn_g[l], post_ffn_g[l], moe_w_gate[i].astype(BF16),
                     moe_w_up[i].astype(BF16), moe_w_down[i].astype(BF16), router=(rw_hi, rw_lo, rb), name="moe_ffn")

        if l == n_gdn - 1:
            k_all = _norm_matmul(x, kv_modg, 0, 1, kv_norm_g, kv_w[:, :fox_w].astype(BF16), out_dtype=F32, name="kv_k_proj")
            v_all = _norm_matmul(x, kv_modg, 0, 1, kv_norm_g, kv_w[:, fox_w:2 * fox_w].astype(BF16), out_dtype=F32,
                                 name="kv_v_proj")
            w_f = jnp.zeros((d, LANES), BF16).at[:, :fox_heads].set(kv_w[:, 2 * fox_w:].astype(BF16))
            b_f = jnp.zeros((LANES,), F32).at[:fox_heads].set(kv_b_f)
            logf = _norm_matmul(x, kv_modg, 0, 1, kv_norm_g, w_f, out_dtype=F32, bias=b_f, name="kv_f_proj")[:, :fox_heads]
            logf_p = logf[:n_p].reshape(bp, tp, fox_heads)
            logf_s = logf[n_p:].reshape(bs, ts, fox_heads)
            f_p = _cumsum_rows(jnp.swapaxes(logf_p, 1, 2))[:, :, None, :]
            rows_s = jnp.swapaxes(jnp.concatenate([cache_logf, logf_s], axis=1), 1, 2)
            t_all = t_cache + ts
            t_pad = -(-t_all // LANES) * LANES
            f_s = _cumsum_rows(jnp.pad(rows_s, ((0, 0), (0, 0), (0, t_pad - t_all))))[:, :, None, :]

    def heads_view(a, lo, hi, b, t):
        return a[lo:hi].reshape(b, t, fox_heads, fox_hd)

    delta = jnp.stack(delta_out, axis=1)
    conv = jnp.stack(conv_out, axis=1)
    return (x[:n_p].reshape(bp, tp, d), x[n_p:].reshape(bs, ts, d),
            heads_view(k_all, 0, n_p, bp, tp), heads_view(v_all, 0, n_p, bp, tp), logf[:n_p].reshape(bp, tp, fox_heads),
            delta[:bp], conv[:bp],
            heads_view(k_all, n_p, n, bs, ts), heads_view(v_all, n_p, n, bs, ts), logf[n_p:].reshape(bs, ts, fox_heads),
            delta[bp:], conv[bp:])
```

```python
import functools

import jax
import jax.numpy as jnp
import numpy as np
from jax import lax
from jax.experimental import pallas as pl
from jax.experimental.pallas import tpu as pltpu

F32 = jnp.float32
BF16 = jnp.bfloat16
NORM_EPS = 1e-6
L2_EPS = 1e-6
GROUP = 64
LANES = 128
SUBLANES = 8
VMEM_LIMIT_BYTES = 56 * 1024 * 1024
CONV_W = 4
TOP_K = 2
NEG_INF = float("-inf")


def _sigmoid(x):
    return 1.0 / (1.0 + jnp.exp(-x))


def _silu(x):
    return x * _sigmoid(x)


def _softplus(x):
    return jnp.maximum(x, 0.0) + jnp.log1p(jnp.exp(-jnp.abs(x)))


def _rms(x, g):
    return x * lax.rsqrt(jnp.mean(x * x, axis=-1, keepdims=True) + NORM_EPS) * g


def _bdot(a, b):
    return jnp.dot(a.astype(BF16), b.astype(BF16), preferred_element_type=F32)


def _bdot_nt(a, b):
    return lax.dot_general(a.astype(BF16), b.astype(BF16), (((1,), (1,)), ((), ())),
                           preferred_element_type=F32)


def _bdot_tn(a, b):
    return lax.dot_general(a.astype(BF16), b.astype(BF16), (((0,), (0,)), ((), ())),
                           preferred_element_type=F32)


def _split2(x):
    hi = x.astype(BF16)
    lo = (x - hi.astype(F32)).astype(BF16)
    return hi, lo


def _dot_split(a, b):
    a_hi, a_lo = _split2(a)
    b_hi, b_lo = _split2(b)
    dot = functools.partial(jnp.dot, preferred_element_type=F32)
    return dot(a_hi, b_hi) + (dot(a_hi, b_lo) + dot(a_lo, b_hi))


def _lane_col(x, idx):
    lane = lax.broadcasted_iota(jnp.int32, x.shape, 1)
    return jnp.sum(jnp.where(lane == idx, x, 0.0), axis=1, keepdims=True)


def _modulated_norm(x_ref, sh_ref, sc_ref, g_ref, groups, emit):
    g = g_ref[...]
    for gi in range(groups):
        rows = pl.ds(gi * GROUP, GROUP)
        h = _rms(x_ref[rows, :], g) * (1.0 + sc_ref[gi:gi + 1, :]) + sh_ref[gi:gi + 1, :]
        emit(rows, h)


def _gated_residual(y_ref, x_ref, gt_ref, g_ref, o_ref, groups):
    g = g_ref[...]
    for gi in range(groups):
        rows = pl.ds(gi * GROUP, GROUP)
        o_ref[rows, :] = x_ref[rows, :] + gt_ref[gi:gi + 1, :] * _rms(y_ref[rows, :], g)


def _tile(n, cap, unit):
    best = None
    for t in range(unit, min(n, cap) + 1, unit):
        if n % t == 0:
            best = t
    assert best is not None, (n, cap, unit)
    return best


def _params(semantics):
    return pltpu.CompilerParams(dimension_semantics=semantics, vmem_limit_bytes=VMEM_LIMIT_BYTES)


def _mod_kernel(c_ref, w_ref, b_ref, o_ref):
    s = _silu(c_ref[...])
    o_ref[...] = _bdot(s, w_ref[...]) + b_ref[...]


def _mod_matmul(c, w, b, tn=1024):
    n_l, d, width = w.shape
    m = c.shape[0]
    tn = _tile(width, tn, LANES)
    return pl.pallas_call(
        _mod_kernel,
        grid=(n_l, width // tn),
        in_specs=[pl.BlockSpec((m, d), lambda l, j: (0, 0)),
                  pl.BlockSpec((None, d, tn), lambda l, j: (l, 0, j)),
                  pl.BlockSpec((None, 1, tn), lambda l, j: (l, 0, j))],
        out_specs=pl.BlockSpec((None, m, tn), lambda l, j: (l, 0, j)),
        out_shape=jax.ShapeDtypeStruct((n_l, m, width), F32),
        compiler_params=_params(("arbitrary", "arbitrary")),
        name="ada_mod",
    )(c, w, b.reshape(n_l, 1, width))


def _norm_mm_kernel(x_ref, sh_ref, sc_ref, g_ref, w_ref, *rest, groups, log_sigmoid):
    if log_sigmoid:
        b_ref, o_ref, h_ref = rest
    else:
        o_ref, h_ref = rest

    @pl.when(pl.program_id(1) == 0)
    def _():
        def emit(rows, h):
            h_ref[rows, :] = h.astype(BF16)
        _modulated_norm(x_ref, sh_ref, sc_ref, g_ref, groups, emit)

    y = jnp.dot(h_ref[...], w_ref[...], preferred_element_type=F32)
    if log_sigmoid:
        y = -_softplus(-(y + b_ref[...]))
    o_ref[...] = y.astype(o_ref.dtype)


def _norm_matmul(x, modg, sh_col, sc_col, g, w, *, out_dtype, bias=None, tm=1024, tn=2048, name):
    n, d = x.shape
    width = w.shape[1]
    tm = _tile(n, tm, GROUP * SUBLANES)
    tn = _tile(width, tn, LANES)
    groups = tm // GROUP
    in_specs = [pl.BlockSpec((tm, d), lambda i, j: (i, 0)),
                pl.BlockSpec((groups, d), lambda i, j: (i, sh_col)),
                pl.BlockSpec((groups, d), lambda i, j: (i, sc_col)),
                pl.BlockSpec((1, d), lambda i, j: (0, 0)),
                pl.BlockSpec((d, tn), lambda i, j: (0, j))]
    args = [x, modg, modg, g.reshape(1, d), w]
    if bias is not None:
        in_specs.append(pl.BlockSpec((1, tn), lambda i, j: (0, j)))
        args.append(bias.reshape(1, width))
    return pl.pallas_call(
        functools.partial(_norm_mm_kernel, groups=groups, log_sigmoid=bias is not None),
        grid=(n // tm, width // tn),
        in_specs=in_specs,
        out_specs=pl.BlockSpec((tm, tn), lambda i, j: (i, j)),
        out_shape=jax.ShapeDtypeStruct((n, width), out_dtype),
        scratch_shapes=[pltpu.VMEM((tm, d), BF16)],
        compiler_params=_params(("arbitrary", "arbitrary")),
        name=name,
    )(*args)


def _mm_post_kernel(a_ref, w_ref, x_ref, gt_ref, g_ref, o_ref, y_ref, *, groups):
    y_ref[...] = jnp.dot(a_ref[...], w_ref[...], preferred_element_type=F32)
    _gated_residual(y_ref, x_ref, gt_ref, g_ref, o_ref, groups)


def _matmul_post(a, w, x, modg, gt_col, g, *, tm=512, name):
    n, d = x.shape
    k = a.shape[1]
    tm = _tile(n, tm, GROUP * SUBLANES)
    groups = tm // GROUP
    return pl.pallas_call(
        functools.partial(_mm_post_kernel, groups=groups),
        grid=(n // tm,),
        in_specs=[pl.BlockSpec((tm, k), lambda i: (i, 0)),
                  pl.BlockSpec((k, d), lambda i: (0, 0)),
                  pl.BlockSpec((tm, d), lambda i: (i, 0)),
                  pl.BlockSpec((groups, d), lambda i: (i, gt_col)),
                  pl.BlockSpec((1, d), lambda i: (0, 0))],
        out_specs=pl.BlockSpec((tm, d), lambda i: (i, 0)),
        out_shape=jax.ShapeDtypeStruct((n, d), F32),
        scratch_shapes=[pltpu.VMEM((tm, d), F32)],
        compiler_params=_params(("arbitrary",)),
        name=name,
    )(a, w, x, modg, g.reshape(1, d))


def _swiglu_step(h, wg_ref, wu_ref, wd_ref):
    gate = jnp.dot(h, wg_ref[...], preferred_element_type=F32)
    up = jnp.dot(h, wu_ref[...], preferred_element_type=F32)
    act = (_silu(gate) * up).astype(BF16)
    return jnp.dot(act, wd_ref[...], preferred_element_type=F32)


def _ffn_kernel(x_ref, sh_ref, sc_ref, gt_ref, gpre_ref, gpost_ref, wg_ref, wu_ref, wd_ref, o_ref, h_ref, acc_ref,
                *, groups):
    j = pl.program_id(1)

    @pl.when(j == 0)
    def _():
        def emit(rows, h):
            h_ref[rows, :] = h.astype(BF16)
        _modulated_norm(x_ref, sh_ref, sc_ref, gpre_ref, groups, emit)
        acc_ref[...] = jnp.zeros_like(acc_ref)

    acc_ref[...] += _swiglu_step(h_ref[...], wg_ref, wu_ref, wd_ref)

    @pl.when(j == pl.num_programs(1) - 1)
    def _():
        _gated_residual(acc_ref, x_ref, gt_ref, gpost_ref, o_ref, groups)


def _ffn(x, modg, g_pre, g_post, wg, wu, wd, *, tm=512, tf=1408, name):
    n, d = x.shape
    f = wg.shape[1]
    tm = _tile(n, tm, GROUP * SUBLANES)
    tf = _tile(f, tf, LANES)
    groups = tm // GROUP
    row = lambda i, j: (i, 0)
    return pl.pallas_call(
        functools.partial(_ffn_kernel, groups=groups),
        grid=(n // tm, f // tf),
        in_specs=[pl.BlockSpec((tm, d), row),
                  pl.BlockSpec((groups, d), lambda i, j: (i, 3)),
                  pl.BlockSpec((groups, d), lambda i, j: (i, 4)),
                  pl.BlockSpec((groups, d), lambda i, j: (i, 5)),
                  pl.BlockSpec((1, d), lambda i, j: (0, 0)),
                  pl.BlockSpec((1, d), lambda i, j: (0, 0)),
                  pl.BlockSpec((d, tf), lambda i, j: (0, j)),
                  pl.BlockSpec((d, tf), lambda i, j: (0, j)),
                  pl.BlockSpec((tf, d), lambda i, j: (j, 0))],
        out_specs=pl.BlockSpec((tm, d), row),
        out_shape=jax.ShapeDtypeStruct((n, d), F32),
        scratch_shapes=[pltpu.VMEM((tm, d), BF16), pltpu.VMEM((tm, d), F32)],
        compiler_params=_params(("arbitrary", "arbitrary")),
        name=name,
    )(x, modg, modg, modg, g_pre.reshape(1, d), g_post.reshape(1, d), wg, wu, wd)


META_I1, META_I2, META_R1, META_R2, META_W1, META_W2 = range(6)


def _top2(logits, n_experts):
    lane = lax.broadcasted_iota(jnp.int32, logits.shape, 1).astype(F32)
    big = float(LANES)
    logits = jnp.where(lane < n_experts, logits, NEG_INF)
    m1 = jnp.max(logits, axis=1, keepdims=True)
    i1 = jnp.min(jnp.where(logits == m1, lane, big), axis=1, keepdims=True)
    rest = jnp.where(lane == i1, NEG_INF, logits)
    m2 = jnp.max(rest, axis=1, keepdims=True)
    i2 = jnp.min(jnp.where(rest == m2, lane, big), axis=1, keepdims=True)
    e2 = jnp.exp(m2 - m1)
    den = 1.0 + e2
    return lane, i1, i2, 1.0 / den, e2 / den


def _route_kernel(x_ref, sh_ref, sc_ref, g_ref, rw_hi_ref, rw_lo_ref, rb_ref,
                  h_ref, meta_ref, cnt_ref, sel_ref, carry_ref, *, groups, n_experts):
    tm = x_ref.shape[0]

    @pl.when(pl.program_id(0) == 0)
    def _():
        carry_ref[...] = jnp.zeros_like(carry_ref)

    def emit(rows, h):
        h_ref[rows, :] = h
        h_hi, h_lo = _split2(h)
        dot = functools.partial(jnp.dot, preferred_element_type=F32)
        logits = dot(h_hi, rw_hi_ref[...]) + (dot(h_hi, rw_lo_ref[...]) + dot(h_lo, rw_hi_ref[...]))
        lane, i1, i2, w1, w2 = _top2(logits + rb_ref[...], n_experts)
        sel_ref[rows, :] = jnp.where((lane == i1) | (lane == i2), 1.0, 0.0).astype(BF16)
        meta_ref[rows, :] = (jnp.where(lane == META_I1, i1, 0.0) + jnp.where(lane == META_I2, i2, 0.0)
                             + jnp.where(lane == META_W1, w1, 0.0) + jnp.where(lane == META_W2, w2, 0.0))
    _modulated_norm(x_ref, sh_ref, sc_ref, g_ref, groups, emit)

    sel = sel_ref[...]
    earlier = (lax.broadcasted_iota(jnp.int32, (tm, tm), 0) > lax.broadcasted_iota(jnp.int32, (tm, tm), 1))
    rank = jnp.dot(earlier.astype(BF16), sel, preferred_element_type=F32) + carry_ref[...]
    meta = meta_ref[...]
    lane = lax.broadcasted_iota(jnp.int32, (tm, LANES), 1).astype(F32)
    r1 = jnp.sum(jnp.where(lane == _lane_col(meta, META_I1), rank, 0.0), axis=1, keepdims=True)
    r2 = jnp.sum(jnp.where(lane == _lane_col(meta, META_I2), rank, 0.0), axis=1, keepdims=True)
    meta_ref[...] = meta + jnp.where(lane == META_R1, r1, 0.0) + jnp.where(lane == META_R2, r2, 0.0)
    carry_ref[...] += jnp.sum(sel.astype(F32), axis=0, keepdims=True)
    cnt_ref[...] = carry_ref[...]


def _moe_route(x, modg, g_pre, router, *, tm=512):
    n, d = x.shape
    rw_hi, rw_lo, rb, n_experts = router
    tm = _tile(n, tm, GROUP * SUBLANES)
    groups = tm // GROUP
    const = lambda i: (0, 0)
    return pl.pallas_call(
        functools.partial(_route_kernel, groups=groups, n_experts=n_experts),
        grid=(n // tm,),
        in_specs=[pl.BlockSpec((tm, d), lambda i: (i, 0)),
                  pl.BlockSpec((groups, d), lambda i: (i, 3)),
                  pl.BlockSpec((groups, d), lambda i: (i, 4)),
                  pl.BlockSpec((1, d), const),
                  pl.BlockSpec((d, LANES), const),
                  pl.BlockSpec((d, LANES), const),
                  pl.BlockSpec((1, LANES), const)],
        out_specs=[pl.BlockSpec((tm, d), lambda i: (i, 0)),
                   pl.BlockSpec((tm, LANES), lambda i: (i, 0)),
                   pl.BlockSpec((1, LANES), const)],
        out_shape=[jax.ShapeDtypeStruct((n, d), F32),
                   jax.ShapeDtypeStruct((n, LANES), F32),
                   jax.ShapeDtypeStruct((1, LANES), F32)],
        scratch_shapes=[pltpu.VMEM((tm, LANES), BF16), pltpu.VMEM((1, LANES), F32)],
        compiler_params=_params(("arbitrary",)),
        name="moe_route",
    )(x, modg, modg, g_pre.reshape(1, d), rw_hi, rw_lo, rb)


def _dispatch_kernel(dest_ref, h_ref, init_ref, hs_ref, sem, *, n_tokens):
    del init_ref
    tb = h_ref.shape[0]
    base = pl.program_id(0) * tb

    def row_copy(r, k):
        slot = dest_ref[k * n_tokens + base + r]
        return pltpu.make_async_copy(h_ref.at[pl.ds(r, 1), :], hs_ref.at[pl.ds(slot, 1), :], sem)

    def start(r, carry):
        for k in range(TOP_K):
            row_copy(r, k).start()
        return carry

    def wait(r, carry):
        for k in range(TOP_K):
            row_copy(r, k).wait()
        return carry

    lax.fori_loop(0, tb, start, 0)
    lax.fori_loop(0, tb, wait, 0)


def _moe_dispatch(h, dest, n_rows, *, tb=512):
    n, d = h.shape
    tb = _tile(n, tb, SUBLANES)
    grid_spec = pltpu.PrefetchScalarGridSpec(
        num_scalar_prefetch=1,
        grid=(n // tb,),
        in_specs=[pl.BlockSpec((tb, d), lambda i, dest: (i, 0)),
                  pl.BlockSpec(memory_space=pl.ANY)],
        out_specs=pl.BlockSpec(memory_space=pl.ANY),
        scratch_shapes=[pltpu.SemaphoreType.DMA(())],
    )
    return pl.pallas_call(
        functools.partial(_dispatch_kernel, n_tokens=n),
        grid_spec=grid_spec,
        out_shape=jax.ShapeDtypeStruct((n_rows, d), h.dtype),
        input_output_aliases={2: 0},
        compiler_params=_params(("arbitrary",)),
        name="moe_dispatch",
    )(dest, h, jnp.zeros((n_rows, d), h.dtype))


def _experts_kernel(tile_expert_ref, n_valid_ref, hs_ref, wg_ref, wu_ref, wd_ref, o_ref, hb_ref, acc_ref):
    del tile_expert_ref
    j = pl.program_id(1)

    @pl.when(pl.program_id(0) < n_valid_ref[0])
    def _():
        @pl.when(j == 0)
        def _():
            hb_ref[...] = hs_ref[...].astype(BF16)
            acc_ref[...] = jnp.zeros_like(acc_ref)

        acc_ref[...] += _swiglu_step(hb_ref[...], wg_ref, wu_ref, wd_ref)

        @pl.when(j == pl.num_programs(1) - 1)
        def _():
            o_ref[...] = acc_ref[...]

    @pl.when(pl.program_id(0) >= n_valid_ref[0])
    def _():
        o_ref[...] = jnp.zeros_like(o_ref)


def _moe_experts(hs, tile_expert, n_valid, wg, wu, wd, *, tr, tf=1408):
    n_rows, d = hs.shape
    f = wg.shape[2]
    tf = _tile(f, tf, LANES)
    grid_spec = pltpu.PrefetchScalarGridSpec(
        num_scalar_prefetch=2,
        grid=(n_rows // tr, f // tf),
        in_specs=[pl.BlockSpec((tr, d), lambda i, j, te, nv: (i, 0)),
                  pl.BlockSpec((None, d, tf), lambda i, j, te, nv: (te[i], 0, j)),
                  pl.BlockSpec((None, d, tf), lambda i, j, te, nv: (te[i], 0, j)),
                  pl.BlockSpec((None, tf, d), lambda i, j, te, nv: (te[i], j, 0))],
        out_specs=pl.BlockSpec((tr, d), lambda i, j, te, nv: (i, 0)),
        scratch_shapes=[pltpu.VMEM((tr, d), BF16), pltpu.VMEM((tr, d), F32)],
    )
    return pl.pallas_call(
        _experts_kernel,
        grid_spec=grid_spec,
        out_shape=jax.ShapeDtypeStruct((n_rows, d), F32),
        compiler_params=_params(("arbitrary", "arbitrary")),
        name="moe_experts",
    )(tile_expert, n_valid, hs, wg, wu, wd)


def _combine_kernel(dest_ref, ys_ref, meta_ref, x_ref, gt_ref, g_ref, o_ref, buf_ref, sem, *, n_tokens, groups):
    tb = x_ref.shape[0]
    base = pl.program_id(0) * tb

    def row_copy(r, k):
        slot = dest_ref[k * n_tokens + base + r]
        return pltpu.make_async_copy(ys_ref.at[pl.ds(slot, 1), :], buf_ref.at[k, pl.ds(r, 1), :], sem)

    def start(r, carry):
        for k in range(TOP_K):
            row_copy(r, k).start()
        return carry

    def wait(r, carry):
        for k in range(TOP_K):
            row_copy(r, k).wait()
        return carry

    lax.fori_loop(0, tb, start, 0)
    lax.fori_loop(0, tb, wait, 0)

    meta = meta_ref[...]
    w1 = _lane_col(meta, META_W1)
    w2 = _lane_col(meta, META_W2)
    g = g_ref[...]
    for gi in range(groups):
        lo, hi = gi * GROUP, (gi + 1) * GROUP
        y = w1[lo:hi] * buf_ref[0, lo:hi, :] + w2[lo:hi] * buf_ref[1, lo:hi, :]
        o_ref[lo:hi, :] = x_ref[lo:hi, :] + gt_ref[gi:gi + 1, :] * _rms(y, g)


def _moe_combine(ys, dest, meta, x, modg, g_post, *, tb=512):
    n, d = x.shape
    tb = _tile(n, tb, GROUP * SUBLANES)
    groups = tb // GROUP
    grid_spec = pltpu.PrefetchScalarGridSpec(
        num_scalar_prefetch=1,
        grid=(n // tb,),
        in_specs=[pl.BlockSpec(memory_space=pl.ANY),
                  pl.BlockSpec((tb, LANES), lambda i, dest: (i, 0)),
                  pl.BlockSpec((tb, d), lambda i, dest: (i, 0)),
                  pl.BlockSpec((groups, d), lambda i, dest: (i, 5)),
                  pl.BlockSpec((1, d), lambda i, dest: (0, 0))],
        out_specs=pl.BlockSpec((tb, d), lambda i, dest: (i, 0)),
        scratch_shapes=[pltpu.VMEM((TOP_K, tb, d), F32), pltpu.SemaphoreType.DMA(())],
    )
    return pl.pallas_call(
        functools.partial(_combine_kernel, n_tokens=n, groups=groups),
        grid_spec=grid_spec,
        out_shape=jax.ShapeDtypeStruct((n, d), F32),
        compiler_params=_params(("arbitrary",)),
        name="moe_combine",
    )(dest, ys, meta, x, modg, g_post.reshape(1, d))


def _moe(x, modg, g_pre, g_post, router, wg, wu, wd, *, tr=512):
    n, d = x.shape
    n_experts = router[3]
    h, meta, counts = _moe_route(x, modg, g_pre, router)

    cnt = counts[0, :n_experts].astype(jnp.int32)
    padded = (cnt + tr - 1) // tr * tr
    ends = jnp.cumsum(padded)
    starts = ends - padded
    n_tiles = -(-TOP_K * n // tr) + n_experts
    experts = jnp.arange(n_experts, dtype=jnp.int32)

    def slot(i_lane, r_lane):
        idx = meta[:, i_lane].astype(jnp.int32)
        seg = jnp.sum(jnp.where(idx[:, None] == experts[None, :], starts[None, :], 0), axis=1)
        return seg + meta[:, r_lane].astype(jnp.int32)

    dest = jnp.concatenate([slot(META_I1, META_R1), slot(META_I2, META_R2)])
    tile_start = jnp.arange(n_tiles, dtype=jnp.int32) * tr
    tile_expert = jnp.minimum(jnp.sum((tile_start[:, None] >= ends[None, :]).astype(jnp.int32), axis=1), n_experts - 1)
    n_valid = (ends[-1] // tr).reshape(1)

    hs = _moe_dispatch(h, dest, n_tiles * tr)
    ys = _moe_experts(hs, tile_expert, n_valid, wg, wu, wd, tr=tr)
    return _moe_combine(ys, dest, meta, x, modg, g_post)


def _unit_lower_inverse(low):
    c = low.shape[0]
    n_blk = c // SUBLANES
    row = lax.broadcasted_iota(jnp.int32, (SUBLANES, c), 0)
    col = lax.broadcasted_iota(jnp.int32, (SUBLANES, c), 1)
    low_blk = [low[r * SUBLANES:(r + 1) * SUBLANES, :] for r in range(n_blk)]
    inv_blk = [jnp.where(col == row + r * SUBLANES, 1.0, 0.0) for r in range(n_blk)]
    for j in range(c - 1):
        rj = j // SUBLANES
        pivot_row = inv_blk[rj][j % SUBLANES:j % SUBLANES + 1, :]
        for r in range(rj, n_blk):
            inv_blk[r] = inv_blk[r] - low_blk[r][:, j:j + 1] * pivot_row
    return jnp.concatenate(inv_blk, axis=0)


def _gdn_kernel(seq_ref, first_ref, last_ref,
                proj_ref, ba_ref, ctx_ref, cw_ref, alog_ref, dtb_ref, gain_ref, s0_ref,
                og_ref, sout_ref, xs_ref, st_ref, *, heads, dk, dv):
    del seq_ref
    i = pl.program_id(0)
    c = GROUP
    pad = SUBLANES
    qk_w = heads * dk
    conv_dim = 2 * qk_w + heads * dv

    @pl.when(first_ref[i] == 1)
    def _():
        xs_ref[0:pad, :] = ctx_ref[...]
        st_ref[...] = s0_ref[...]

    @pl.when(first_ref[i] == 0)
    def _():
        xs_ref[0:pad, :] = xs_ref[c:c + pad, :]

    xs_ref[pad:pad + c, :] = proj_ref[:, 0:conv_dim]

    row = lax.broadcasted_iota(jnp.int32, (c, c), 0)
    col = lax.broadcasted_iota(jnp.int32, (c, c), 1)
    strict = row > col
    causal = row >= col
    eye = row == col
    ba = ba_ref[...]
    gain = gain_ref[...]

    def conv(c0):
        acc = None
        for t in range(CONV_W):
            start = pad - (CONV_W - 1) + t
            term = xs_ref[start:start + c, c0:c0 + LANES] * cw_ref[t:t + 1, c0:c0 + LANES]
            acc = term if acc is None else acc + term
        return _silu(acc)

    for hh in range(heads):
        q = conv(hh * dk)
        k = conv(qk_w + hh * dk)
        v = conv(2 * qk_w + hh * dv)
        q = q * lax.rsqrt(jnp.sum(q * q, axis=-1, keepdims=True) + L2_EPS) * (dk ** -0.5)
        k = k * lax.rsqrt(jnp.sum(k * k, axis=-1, keepdims=True) + L2_EPS)
        beta = _sigmoid(_lane_col(ba, hh))
        g = -jnp.exp(_lane_col(alog_ref[...], hh)) * _softplus(_lane_col(ba, heads + hh)
                                                                + _lane_col(dtb_ref[...], hh))
        g_row = jnp.sum(jnp.where(eye, g, 0.0), axis=0, keepdims=True)
        gc_col = jnp.sum(jnp.where(causal, g_row, 0.0), axis=1, keepdims=True)
        gc_row = jnp.sum(jnp.where(row <= col, g, 0.0), axis=0, keepdims=True)
        diff = gc_col - gc_row
        kb = k * beta
        lower = jnp.where(strict, _bdot_nt(kb, k) * jnp.exp(jnp.where(strict, diff, 0.0)), 0.0)
        a_qk = jnp.where(causal, _bdot_nt(q, k) * jnp.exp(jnp.where(causal, diff, 0.0)), 0.0)
        inv = _unit_lower_inverse(lower)
        eg = jnp.exp(gc_col)
        uw = _dot_split(inv, jnp.concatenate([v * beta, kb * eg], axis=1))
        u = uw[:, :dv]
        w = uw[:, dv:]
        gc_last = gc_col[c - 1:c, :]
        q_dec = q * eg
        k_tail = k * jnp.exp(gc_last - gc_col)
        s = st_ref[hh]
        ws_qs = _bdot(jnp.concatenate([w, q_dec], axis=0), s)
        v_new = u - ws_qs[:c]
        o = ws_qs[c:] + _bdot(a_qk, v_new)
        st_ref[hh] = s * jnp.exp(gc_last) + _bdot_tn(k_tail, v_new)
        z = proj_ref[:, conv_dim + hh * dv:conv_dim + (hh + 1) * dv]
        og_ref[:, hh * dv:(hh + 1) * dv] = (_rms(o, gain) * _silu(z)).astype(og_ref.dtype)

    @pl.when(last_ref[i] == 1)
    def _():
        sout_ref[...] = st_ref[...]


def _gdn_core(proj, ba, ctx8, conv_w, a_log, dt_bias, o_gain, s0, seq_of_chunk, first, last, *, heads, dk, dv):
    n, pw = proj.shape
    conv_dim = 2 * heads * dk + heads * dv
    n_seq = s0.shape[0]
    c = GROUP
    assert n % c == 0 and dk == LANES and dv == LANES

    def lane_row(vec):
        return jnp.zeros((1, LANES), F32).at[0, :vec.shape[0]].set(vec.astype(F32))

    grid_spec = pltpu.PrefetchScalarGridSpec(
        num_scalar_prefetch=3,
        grid=(n // c,),
        in_specs=[pl.BlockSpec((c, pw), lambda i, s, f, l: (i, 0)),
                  pl.BlockSpec((c, LANES), lambda i, s, f, l: (i, 0)),
                  pl.BlockSpec((None, SUBLANES, conv_dim), lambda i, s, f, l: (s[i], 0, 0)),
                  pl.BlockSpec((CONV_W, conv_dim), lambda i, s, f, l: (0, 0)),
                  pl.BlockSpec((1, LANES), lambda i, s, f, l: (0, 0)),
                  pl.BlockSpec((1, LANES), lambda i, s, f, l: (0, 0)),
                  pl.BlockSpec((1, dv), lambda i, s, f, l: (0, 0)),
                  pl.BlockSpec((None, heads, dk, dv), lambda i, s, f, l: (s[i], 0, 0, 0))],
        out_specs=[pl.BlockSpec((c, heads * dv), lambda i, s, f, l: (i, 0)),
                   pl.BlockSpec((None, heads, dk, dv), lambda i, s, f, l: (s[i], 0, 0, 0))],
        scratch_shapes=[pltpu.VMEM((c + SUBLANES, conv_dim), F32),
                        pltpu.VMEM((heads, dk, dv), F32)],
    )
    return pl.pallas_call(
        functools.partial(_gdn_kernel, heads=heads, dk=dk, dv=dv),
        grid_spec=grid_spec,
        out_shape=[jax.ShapeDtypeStruct((n, heads * dv), BF16),
                   jax.ShapeDtypeStruct((n_seq, heads, dk, dv), F32)],
        compiler_params=_params(("arbitrary",)),
        name="gdn_core",
    )(seq_of_chunk, first, last, proj, ba, ctx8, conv_w, lane_row(a_log), lane_row(dt_bias),
      o_gain.reshape(1, dv).astype(F32), s0)


def _cumsum_kernel(x_ref, o_ref):
    heads, t = x_ref.shape
    upper = (lax.broadcasted_iota(jnp.int32, (LANES, LANES), 0)
             <= lax.broadcasted_iota(jnp.int32, (LANES, LANES), 1)).astype(BF16)

    def body(i, carry):
        cols = pl.ds(pl.multiple_of(i * LANES, LANES), LANES)
        x = x_ref[:, cols]
        x1 = x.astype(BF16)
        r1 = x - x1.astype(F32)
        x2 = r1.astype(BF16)
        x3 = (r1 - x2.astype(F32)).astype(BF16)
        dot = functools.partial(jnp.dot, preferred_element_type=F32)
        out = carry + (dot(x1, upper) + (dot(x2, upper) + dot(x3, upper)))
        o_ref[:, cols] = out
        return out[:, LANES - 1:LANES]

    lax.fori_loop(0, t // LANES, body, jnp.zeros((heads, 1), F32))


def _cumsum_rows(x):
    b, heads, t = x.shape
    assert t % LANES == 0
    return pl.pallas_call(
        _cumsum_kernel,
        grid=(b,),
        in_specs=[pl.BlockSpec((None, heads, t), lambda i: (i, 0, 0))],
        out_specs=pl.BlockSpec((None, heads, t), lambda i: (i, 0, 0)),
        out_shape=jax.ShapeDtypeStruct((b, heads, t), F32),
        compiler_params=_params(("arbitrary",)),
        name="logf_cumsum",
    )(x)


def _fox_kernel(q_ref, kn_ref, vn_ref, *rest, tq, tc, n_cache, hd):
    if n_cache:
        kc_ref, vc_ref, f_ref, o_ref, kb_ref, vb_ref = rest
    else:
        f_ref, o_ref, kb_ref, vb_ref = rest
    qi = pl.program_id(2)
    q_off = n_cache * tc

    @pl.when(qi == 0)
    def _():
        kb_ref[...] = kn_ref[...].astype(BF16)
        vb_ref[...] = vn_ref[...].astype(BF16)

    lane = lax.broadcasted_iota(jnp.int32, (tq, LANES), 1)
    row = lax.broadcasted_iota(jnp.int32, (tq, tq), 0)
    col = lax.broadcasted_iota(jnp.int32, (tq, tq), 1)
    q2 = q_ref[...] * (hd ** -0.5)
    single_q_tile = kn_ref.shape[0] == tq
    q_start = 0 if single_q_tile else pl.multiple_of(qi * tq, tq)
    halves = range(LANES // hd)
    in_half = [(lane >= h * hd) & (lane < (h + 1) * hd) for h in halves]
    q_stack = jnp.concatenate([jnp.where(in_half[h], q2, jnp.zeros_like(q2)) for h in halves], axis=0)
    f_q_rows = [f_ref[h, :, pl.ds(q_off + q_start, tq)] for h in halves]
    f_q = [jnp.sum(jnp.where(row == col, f_q_rows[h], 0.0), axis=1, keepdims=True) for h in halves]

    def tile(carry, k_t, v_t, f_k_rows, mask):
        s_all = lax.dot_general(q_stack, k_t, (((1,), (1,)), ((), ())), preferred_element_type=F32)
        stats, probs = [], []
        for h in halves:
            m, l, _ = carry[h]
            s = s_all[h * tq:(h + 1) * tq] + (f_q[h] - f_k_rows[h])
            if mask is not None:
                s = jnp.where(mask, s, NEG_INF)
            m_new = jnp.maximum(m, jnp.max(s, axis=1, keepdims=True))
            alpha = jnp.exp(m - m_new)
            p = jnp.exp(s - m_new)
            stats.append((m_new, alpha * l + jnp.sum(p, axis=1, keepdims=True), alpha))
            probs.append(p.astype(BF16))
        pv = jnp.dot(jnp.concatenate(probs, axis=0), v_t, preferred_element_type=F32)
        return tuple((stats[h][0], stats[h][1], stats[h][2] * carry[h][2] + pv[h * tq:(h + 1) * tq]) for h in halves)

    carry = tuple((jnp.full((tq, 1), NEG_INF, F32), jnp.zeros((tq, 1), F32), jnp.zeros((tq, LANES), F32))
                  for _ in halves)

    if n_cache:
        def cache_body(t, carry):
            rows = pl.ds(pl.multiple_of(t * tc, tc), tc)
            return tile(carry, kc_ref[rows, :].astype(BF16), vc_ref[rows, :].astype(BF16),
                        [f_ref[h, :, rows] for h in halves], None)
        carry = lax.fori_loop(0, n_cache, cache_body, carry)

    def new_body(t, carry):
        start = pl.multiple_of(t * tq, tq)
        rows = pl.ds(start, tq)
        return tile(carry, kb_ref[rows, :], vb_ref[rows, :],
                    [f_ref[h, :, pl.ds(q_off + start, tq)] for h in halves], None)
    if not single_q_tile:
        carry = lax.fori_loop(0, qi, new_body, carry)

    rows = pl.ds(q_start, tq)
    carry = tile(carry, kb_ref[rows, :], vb_ref[rows, :], f_q_rows, row >= col)

    out = carry[0][2] / carry[0][1]
    for h in list(halves)[1:]:
        out = jnp.where(in_half[h], carry[h][2] / carry[h][1], out)
    o_ref[...] = out.astype(o_ref.dtype)


def _fox_attention(q, k_new, v_new, f_rows, row0, n_batch, t_new, cache=None, *, heads, hd, tq, tc=512):
    width = heads * hd
    per_blk = LANES // hd
    assert t_new % tq == 0 and row0 % t_new == 0 and row0 % tq == 0
    assert tq % LANES == 0 or tq == t_new
    n_q = t_new // tq
    n_cache = 0
    in_specs = [pl.BlockSpec((tq, LANES), lambda b, h, i: (row0 // tq + b * n_q + i, h)),
                pl.BlockSpec((t_new, LANES), lambda b, h, i: (row0 // t_new + b, h)),
                pl.BlockSpec((t_new, LANES), lambda b, h, i: (row0 // t_new + b, h))]
    args = [q, k_new, v_new]
    if cache is not None:
        t_cache = cache[0].shape[1]
        tc = _tile(t_cache, tc, LANES)
        n_cache = t_cache // tc
        in_specs += [pl.BlockSpec((None, t_cache, LANES), lambda b, h, i: (b, 0, h)),
                     pl.BlockSpec((None, t_cache, LANES), lambda b, h, i: (b, 0, h))]
        args += list(cache)
    t_pad = f_rows.shape[-1]
    in_specs.append(pl.BlockSpec((None, per_blk, 1, t_pad), lambda b, h, i: (b, h, 0, 0)))
    args.append(f_rows)
    return pl.pallas_call(
        functools.partial(_fox_kernel, tq=tq, tc=tc, n_cache=n_cache, hd=hd),
        grid=(n_batch, width // LANES, n_q),
        in_specs=in_specs,
        out_specs=pl.BlockSpec((tq, LANES), lambda b, h, i: (b * n_q + i, h)),
        out_shape=jax.ShapeDtypeStruct((n_batch * t_new, width), BF16),
        scratch_shapes=[pltpu.VMEM((t_new, LANES), BF16), pltpu.VMEM((t_new, LANES), BF16)],
        compiler_params=_params(("arbitrary", "arbitrary", "arbitrary")),
        name="fox_cache" if cache is not None else "fox_prompt",
    )(*args)


def kernel(x_prompt, x_sample, c_prompt, c_sample, cache_k, cache_v, cache_logf, state_delta, state_conv, ada_w, ada_b, pre_mix_g, post_mix_g, pre_ffn_g, post_ffn_g, gdn_w_in, gdn_conv_w, gdn_a_log, gdn_dt_bias, gdn_o_gain, gdn_w_out, kv_norm_g, kv_ada_w, kv_ada_b, kv_w, kv_b_f, fox_w_q, fox_w_o, ffn_w_gate, ffn_w_up, ffn_w_down, moe_router_w, moe_router_b, moe_w_gate, moe_w_up, moe_w_down):
    bp, tp, d = x_prompt.shape
    bs, ts, _ = x_sample.shape
    n_p, n_s = bp * tp, bs * ts
    n = n_p + n_s
    depth = ada_w.shape[0]
    n_gdn = gdn_w_in.shape[0]
    gdn_heads, gdn_dk, gdn_dv = state_delta.shape[2:]
    qk_w, v_w = gdn_heads * gdn_dk, gdn_heads * gdn_dv
    conv_dim = 2 * qk_w + v_w
    fox_heads, fox_hd = cache_k.shape[2:]
    fox_w = fox_heads * fox_hd
    t_cache = cache_k.shape[1]
    n_experts = moe_router_w.shape[-1]
    assert tp % GROUP == 0 and ts % GROUP == 0

    x = jnp.concatenate([x_prompt.reshape(n_p, d), x_sample.reshape(n_s, d)], axis=0)
    c = jnp.concatenate([c_prompt, c_sample], axis=0)

    group_batch = np.concatenate([np.repeat(np.arange(bp), tp // GROUP),
                                  bp + np.repeat(np.arange(bs), ts // GROUP)])

    def per_group(m):
        n_l, _, width = m.shape
        rep = lambda a, b, t: jnp.broadcast_to(a[:, :, None, :], (n_l, b, t // GROUP, width)).reshape(n_l, -1, width)
        return jnp.concatenate([rep(m[:, :bp], bp, tp), rep(m[:, bp:], bs, ts)], axis=1)

    modg = per_group(_mod_matmul(c, ada_w, ada_b))
    kv_modg = per_group(_mod_matmul(c, kv_ada_w[None], kv_ada_b[None]))[0]

    seq_of_chunk = jnp.asarray(group_batch, jnp.int32)
    chunk_in_seq = np.concatenate([np.tile(np.arange(tp // GROUP), bp), np.tile(np.arange(ts // GROUP), bs)])
    seq_chunks = np.concatenate([np.full(n_p // GROUP, tp // GROUP), np.full(n_s // GROUP, ts // GROUP)])
    first = jnp.asarray(chunk_in_seq == 0, jnp.int32)
    last = jnp.asarray(chunk_in_seq == seq_chunks - 1, jnp.int32)

    delta_out, conv_out = [], []
    k_all = v_all = logf = f_p = f_s = None
    for l in range(depth):
        g_mix = pre_mix_g[l]
        if l < n_gdn:
            w_in = gdn_w_in[l]
            w_qkvz = w_in[:, :conv_dim + v_w].astype(BF16)
            w_ba = jnp.zeros((d, LANES), BF16).at[:, :2 * gdn_heads].set(w_in[:, conv_dim + v_w:].astype(BF16))
            proj = _norm_matmul(x, modg[l], 0, 1, g_mix, w_qkvz, out_dtype=F32, name="gdn_in_proj")
            ba = _norm_matmul(x, modg[l], 0, 1, g_mix, w_ba, out_dtype=F32, name="gdn_gate_proj")
            ctx = jnp.concatenate([jnp.zeros((bp, CONV_W - 1, conv_dim), F32), state_conv[:, l]], axis=0)
            ctx8 = jnp.pad(ctx, ((0, 0), (SUBLANES - (CONV_W - 1), 0), (0, 0)))
            s0 = jnp.concatenate([jnp.zeros((bp,) + state_delta.shape[2:], F32), state_delta[:, l]], axis=0)
            mixed, s_out = _gdn_core(proj, ba, ctx8, gdn_conv_w[l], gdn_a_log[l], gdn_dt_bias[l], gdn_o_gain[l],
                                     s0, seq_of_chunk, first, last, heads=gdn_heads, dk=gdn_dk, dv=gdn_dv)
            delta_out.append(s_out)
            tail = lambda a, b, t: a.reshape(b, t, -1)[:, t - (CONV_W - 1):, :conv_dim]
            conv_out.append(jnp.concatenate([tail(proj[:n_p], bp, tp), tail(proj[n_p:], bs, ts)], axis=0))
            w_o = gdn_w_out[l].astype(BF16)
        else:
            j = l - n_gdn
            q = _norm_matmul(x, modg[l], 0, 1, g_mix, fox_w_q[j].astype(BF16), out_dtype=BF16, name="fox_q_proj")
            o_p = _fox_attention(q, k_all, v_all, f_p, 0, bp, tp, heads=fox_heads, hd=fox_hd, tq=min(tp, 256))
            o_s = _fox_attention(q, k_all, v_all, f_s, n_p, bs, ts,
                                 cache=(cache_k.reshape(bs, t_cache, fox_w), cache_v.reshape(bs, t_cache, fox_w)),
                                 heads=fox_heads, hd=fox_hd, tq=ts)
            mixed = jnp.concatenate([o_p, o_s], axis=0)
            w_o = fox_w_o[j].astype(BF16)
        x = _matmul_post(mixed, w_o, x, modg[l], 2, post_mix_g[l], name="mixer_out_proj")

        i = l // 2
        if l % 2 == 0:
            x = _ffn(x, modg[l], pre_ffn_g[l], post_ffn_g[l], ffn_w_gate[i].astype(BF16),
                     ffn_w_up[i].astype(BF16), ffn_w_down[i].astype(BF16), name="dense_ffn")
        else:
            rw = jnp.zeros((d, LANES), F32).at[:, :n_experts].set(moe_router_w[i])
            rw_hi = rw.astype(BF16)
            rw_lo = (rw - rw_hi.astype(F32)).astype(BF16)
            rb = jnp.zeros((1, LANES), F32).at[0, :n_experts].set(moe_router_b[i])
            x = _moe(x, modg[l], pre_ffn_g[l], post_ffn_g[l], (rw_hi, rw_lo, rb, n_experts),
                     moe_w_gate[i].astype(BF16), moe_w_up[i].astype(BF16), moe_w_down[i].astype(BF16))

        if l == n_gdn - 1:
            k_all = _norm_matmul(x, kv_modg, 0, 1, kv_norm_g, kv_w[:, :fox_w].astype(BF16), out_dtype=F32, name="kv_k_proj")
            v_all = _norm_matmul(x, kv_modg, 0, 1, kv_norm_g, kv_w[:, fox_w:2 * fox_w].astype(BF16), out_dtype=F32,
                                 name="kv_v_proj")
            w_f = jnp.zeros((d, LANES), BF16).at[:, :fox_heads].set(kv_w[:, 2 * fox_w:].astype(BF16))
            b_f = jnp.zeros((LANES,), F32).at[:fox_heads].set(kv_b_f)
            logf = _norm_matmul(x, kv_modg, 0, 1, kv_norm_g, w_f, out_dtype=F32, bias=b_f, name="kv_f_proj")[:, :fox_heads]
            logf_p = logf[:n_p].reshape(bp, tp, fox_heads)
            logf_s = logf[n_p:].reshape(bs, ts, fox_heads)
            f_p = _cumsum_rows(jnp.swapaxes(logf_p, 1, 2))[:, :, None, :]
            rows_s = jnp.swapaxes(jnp.concatenate([cache_logf, logf_s], axis=1), 1, 2)
            t_all = t_cache + ts
            t_pad = -(-t_all // LANES) * LANES
            f_s = _cumsum_rows(jnp.pad(rows_s, ((0, 0), (0, 0), (0, t_pad - t_all))))[:, :, None, :]

    def heads_view(a, lo, hi, b, t):
        return a[lo:hi].reshape(b, t, fox_heads, fox_hd)

    delta = jnp.stack(delta_out, axis=1)
    conv = jnp.stack(conv_out, axis=1)
    return (x[:n_p].reshape(bp, tp, d), x[n_p:].reshape(bs, ts, d),
            heads_view(k_all, 0, n_p, bp, tp), heads_view(v_all, 0, n_p, bp, tp), logf[:n_p].reshape(bp, tp, fox_heads),
            delta[:bp], conv[:bp],
            heads_view(k_all, n_p, n, bs, ts), heads_view(v_all, n_p, n, bs, ts), logf[n_p:].reshape(bs, ts, fox_heads),
            delta[bp:], conv[bp:])
```

```python
import functools
import math

import jax
import jax.numpy as jnp
import numpy as np
from jax import lax
from jax.experimental import pallas as pl
from jax.experimental.pallas import tpu as pltpu

F32 = jnp.float32
BF16 = jnp.bfloat16
NORM_EPS = 1e-6
L2_EPS = 1e-6
GROUP = 64
LANES = 128
SUBLANES = 8
VMEM_LIMIT_BYTES = 56 * 1024 * 1024
CONV_W = 4
TOP_K = 2
NEG_INF = float("-inf")
PAIR_UNROLL = True
DMA_LOOP_UNROLL = 8


def _sigmoid(x):
    return 1.0 / (1.0 + jnp.exp(-x))


def _silu(x):
    return x * _sigmoid(x)


def _softplus(x):
    return jnp.maximum(x, 0.0) + jnp.log1p(jnp.exp(-jnp.abs(x)))


def _rms(x, g):
    return x * lax.rsqrt(jnp.mean(x * x, axis=-1, keepdims=True) + NORM_EPS) * g


def _bdot(a, b):
    return jnp.dot(a.astype(BF16), b.astype(BF16), preferred_element_type=F32)


def _bdot_nt(a, b):
    return lax.dot_general(a.astype(BF16), b.astype(BF16), (((1,), (1,)), ((), ())),
                           preferred_element_type=F32)


def _bdot_tn(a, b):
    return lax.dot_general(a.astype(BF16), b.astype(BF16), (((0,), (0,)), ((), ())),
                           preferred_element_type=F32)


def _split2(x):
    hi = x.astype(BF16)
    lo = (x - hi.astype(F32)).astype(BF16)
    return hi, lo


def _dot_split(a, b):
    a_hi, a_lo = _split2(a)
    b_hi, b_lo = _split2(b)
    dot = functools.partial(jnp.dot, preferred_element_type=F32)
    return dot(a_hi, b_hi) + (dot(a_hi, b_lo) + dot(a_lo, b_hi))


def _lane_col(x, idx):
    lane = lax.broadcasted_iota(jnp.int32, x.shape, 1)
    return jnp.sum(jnp.where(lane == idx, x, 0.0), axis=1, keepdims=True)


def _modulated_norm(x_ref, sh_ref, sc_ref, g_ref, groups, emit):
    g = g_ref[...]
    for gi in range(groups):
        rows = pl.ds(gi * GROUP, GROUP)
        h = _rms(x_ref[rows, :], g) * (1.0 + sc_ref[gi:gi + 1, :]) + sh_ref[gi:gi + 1, :]
        emit(rows, h)


def _gated_residual(y_ref, x_ref, gt_ref, g_ref, o_ref, groups):
    g = g_ref[...]
    for gi in range(groups):
        rows = pl.ds(gi * GROUP, GROUP)
        o_ref[rows, :] = x_ref[rows, :] + gt_ref[gi:gi + 1, :] * _rms(y_ref[rows, :], g)


def _tile(n, cap, unit):
    best = None
    for t in range(unit, min(n, cap) + 1, unit):
        if n % t == 0:
            best = t
    assert best is not None, (n, cap, unit)
    return best


def _params(semantics):
    return pltpu.CompilerParams(dimension_semantics=semantics, vmem_limit_bytes=VMEM_LIMIT_BYTES)


def _mod_kernel(c_ref, w_ref, b_ref, o_ref):
    s = _silu(c_ref[...])
    o_ref[...] = _bdot(s, w_ref[...]) + b_ref[...]


def _mod_matmul(c, w, b, tn=1024):
    n_l, d, width = w.shape
    m = c.shape[0]
    tn = _tile(width, tn, LANES)
    return pl.pallas_call(
        _mod_kernel,
        grid=(n_l, width // tn),
        in_specs=[pl.BlockSpec((m, d), lambda l, j: (0, 0)),
                  pl.BlockSpec((None, d, tn), lambda l, j: (l, 0, j)),
                  pl.BlockSpec((None, 1, tn), lambda l, j: (l, 0, j))],
        out_specs=pl.BlockSpec((None, m, tn), lambda l, j: (l, 0, j)),
        out_shape=jax.ShapeDtypeStruct((n_l, m, width), F32),
        compiler_params=_params(("arbitrary", "arbitrary")),
        name="ada_mod",
    )(c, w, b.reshape(n_l, 1, width))


def _norm_mm_kernel(x_ref, sh_ref, sc_ref, g_ref, w_ref, *rest, groups, log_sigmoid):
    if log_sigmoid:
        b_ref, o_ref, h_ref = rest
    else:
        o_ref, h_ref = rest

    @pl.when(pl.program_id(1) == 0)
    def _():
        def emit(rows, h):
            h_ref[rows, :] = h.astype(BF16)
        _modulated_norm(x_ref, sh_ref, sc_ref, g_ref, groups, emit)

    y = jnp.dot(h_ref[...], w_ref[...], preferred_element_type=F32)
    if log_sigmoid:
        y = -_softplus(-(y + b_ref[...]))
    o_ref[...] = y.astype(o_ref.dtype)


def _norm_matmul(x, modg, sh_col, sc_col, g, w, *, out_dtype, bias=None, rows=None, tm=1024, tn=2048, name):
    d = x.shape[1]
    row0, n = rows if rows is not None else (0, x.shape[0])
    width = w.shape[1]
    tm = _tile(math.gcd(n, row0) if row0 else n, tm, GROUP * SUBLANES)
    tn = _tile(width, tn, LANES)
    groups = tm // GROUP
    t0 = row0 // tm
    in_specs = [pl.BlockSpec((tm, d), lambda i, j: (t0 + i, 0)),
                pl.BlockSpec((groups, d), lambda i, j: (t0 + i, sh_col)),
                pl.BlockSpec((groups, d), lambda i, j: (t0 + i, sc_col)),
                pl.BlockSpec((1, d), lambda i, j: (0, 0)),
                pl.BlockSpec((d, tn), lambda i, j: (0, j))]
    args = [x, modg, modg, g.reshape(1, d), w]
    if bias is not None:
        in_specs.append(pl.BlockSpec((1, tn), lambda i, j: (0, j)))
        args.append(bias.reshape(1, width))
    return pl.pallas_call(
        functools.partial(_norm_mm_kernel, groups=groups, log_sigmoid=bias is not None),
        grid=(n // tm, width // tn),
        in_specs=in_specs,
        out_specs=pl.BlockSpec((tm, tn), lambda i, j: (i, j)),
        out_shape=jax.ShapeDtypeStruct((n, width), out_dtype),
        scratch_shapes=[pltpu.VMEM((tm, d), BF16)],
        compiler_params=_params(("arbitrary", "arbitrary")),
        name=name,
    )(*args)


def _mm_post_kernel(a_ref, w_ref, x_ref, gt_ref, g_ref, o_ref, y_ref, *, groups):
    y_ref[...] = jnp.dot(a_ref[...], w_ref[...], preferred_element_type=F32)
    _gated_residual(y_ref, x_ref, gt_ref, g_ref, o_ref, groups)


def _matmul_post(a, w, x, modg, gt_col, g, *, tm=512, name):
    n, d = x.shape
    k = a.shape[1]
    tm = _tile(n, tm, GROUP * SUBLANES)
    groups = tm // GROUP
    return pl.pallas_call(
        functools.partial(_mm_post_kernel, groups=groups),
        grid=(n // tm,),
        in_specs=[pl.BlockSpec((tm, k), lambda i: (i, 0)),
                  pl.BlockSpec((k, d), lambda i: (0, 0)),
                  pl.BlockSpec((tm, d), lambda i: (i, 0)),
                  pl.BlockSpec((groups, d), lambda i: (i, gt_col)),
                  pl.BlockSpec((1, d), lambda i: (0, 0))],
        out_specs=pl.BlockSpec((tm, d), lambda i: (i, 0)),
        out_shape=jax.ShapeDtypeStruct((n, d), F32),
        scratch_shapes=[pltpu.VMEM((tm, d), F32)],
        compiler_params=_params(("arbitrary",)),
        name=name,
    )(a, w, x, modg, g.reshape(1, d))


def _swiglu_step(h, wg_ref, wu_ref, wd_ref):
    gate = jnp.dot(h, wg_ref[...], preferred_element_type=F32)
    up = jnp.dot(h, wu_ref[...], preferred_element_type=F32)
    act = (_silu(gate) * up).astype(BF16)
    return jnp.dot(act, wd_ref[...], preferred_element_type=F32)


def _ffn_kernel(x_ref, sh_ref, sc_ref, gt_ref, gpre_ref, gpost_ref, wg_ref, wu_ref, wd_ref, o_ref, h_ref, acc_ref,
                *, groups):
    j = pl.program_id(1)

    @pl.when(j == 0)
    def _():
        def emit(rows, h):
            h_ref[rows, :] = h.astype(BF16)
        _modulated_norm(x_ref, sh_ref, sc_ref, gpre_ref, groups, emit)
        acc_ref[...] = jnp.zeros_like(acc_ref)

    acc_ref[...] += _swiglu_step(h_ref[...], wg_ref, wu_ref, wd_ref)

    @pl.when(j == pl.num_programs(1) - 1)
    def _():
        _gated_residual(acc_ref, x_ref, gt_ref, gpost_ref, o_ref, groups)


def _ffn(x, modg, g_pre, g_post, wg, wu, wd, *, tm=512, tf=1408, name):
    n, d = x.shape
    f = wg.shape[1]
    tm = _tile(n, tm, GROUP * SUBLANES)
    tf = _tile(f, tf, LANES)
    groups = tm // GROUP
    row = lambda i, j: (i, 0)
    return pl.pallas_call(
        functools.partial(_ffn_kernel, groups=groups),
        grid=(n // tm, f // tf),
        in_specs=[pl.BlockSpec((tm, d), row),
                  pl.BlockSpec((groups, d), lambda i, j: (i, 3)),
                  pl.BlockSpec((groups, d), lambda i, j: (i, 4)),
                  pl.BlockSpec((groups, d), lambda i, j: (i, 5)),
                  pl.BlockSpec((1, d), lambda i, j: (0, 0)),
                  pl.BlockSpec((1, d), lambda i, j: (0, 0)),
                  pl.BlockSpec((d, tf), lambda i, j: (0, j)),
                  pl.BlockSpec((d, tf), lambda i, j: (0, j)),
                  pl.BlockSpec((tf, d), lambda i, j: (j, 0))],
        out_specs=pl.BlockSpec((tm, d), row),
        out_shape=jax.ShapeDtypeStruct((n, d), F32),
        scratch_shapes=[pltpu.VMEM((tm, d), BF16), pltpu.VMEM((tm, d), F32)],
        compiler_params=_params(("arbitrary", "arbitrary")),
        name=name,
    )(x, modg, modg, modg, g_pre.reshape(1, d), g_post.reshape(1, d), wg, wu, wd)


META_I1, META_I2, META_R1, META_R2, META_W1, META_W2 = range(6)


def _top2(logits, n_experts):
    lane = lax.broadcasted_iota(jnp.int32, logits.shape, 1).astype(F32)
    big = float(LANES)
    logits = jnp.where(lane < n_experts, logits, NEG_INF)
    m1 = jnp.max(logits, axis=1, keepdims=True)
    i1 = jnp.min(jnp.where(logits == m1, lane, big), axis=1, keepdims=True)
    rest = jnp.where(lane == i1, NEG_INF, logits)
    m2 = jnp.max(rest, axis=1, keepdims=True)
    i2 = jnp.min(jnp.where(rest == m2, lane, big), axis=1, keepdims=True)
    e2 = jnp.exp(m2 - m1)
    den = 1.0 + e2
    return lane, i1, i2, 1.0 / den, e2 / den


def _route_kernel(x_ref, sh_ref, sc_ref, g_ref, rw_hi_ref, rw_lo_ref, rb_ref,
                  h_ref, meta_ref, cnt_ref, sel_ref, carry_ref, *, groups, n_experts):
    tm = x_ref.shape[0]

    @pl.when(pl.program_id(0) == 0)
    def _():
        carry_ref[...] = jnp.zeros_like(carry_ref)

    def emit(rows, h):
        h_ref[rows, :] = h
        h_hi, h_lo = _split2(h)
        dot = functools.partial(jnp.dot, preferred_element_type=F32)
        logits = dot(h_hi, rw_hi_ref[...]) + (dot(h_hi, rw_lo_ref[...]) + dot(h_lo, rw_hi_ref[...]))
        lane, i1, i2, w1, w2 = _top2(logits + rb_ref[...], n_experts)
        sel_ref[rows, :] = jnp.where((lane == i1) | (lane == i2), 1.0, 0.0).astype(BF16)
        meta_ref[rows, :] = (jnp.where(lane == META_I1, i1, 0.0) + jnp.where(lane == META_I2, i2, 0.0)
                             + jnp.where(lane == META_W1, w1, 0.0) + jnp.where(lane == META_W2, w2, 0.0))
    _modulated_norm(x_ref, sh_ref, sc_ref, g_ref, groups, emit)

    sel = sel_ref[...]
    earlier = (lax.broadcasted_iota(jnp.int32, (tm, tm), 0) > lax.broadcasted_iota(jnp.int32, (tm, tm), 1))
    rank = jnp.dot(earlier.astype(BF16), sel, preferred_element_type=F32) + carry_ref[...]
    meta = meta_ref[...]
    lane = lax.broadcasted_iota(jnp.int32, (tm, LANES), 1).astype(F32)
    r1 = jnp.sum(jnp.where(lane == _lane_col(meta, META_I1), rank, 0.0), axis=1, keepdims=True)
    r2 = jnp.sum(jnp.where(lane == _lane_col(meta, META_I2), rank, 0.0), axis=1, keepdims=True)
    meta_ref[...] = meta + jnp.where(lane == META_R1, r1, 0.0) + jnp.where(lane == META_R2, r2, 0.0)
    carry_ref[...] += jnp.sum(sel.astype(F32), axis=0, keepdims=True)
    cnt_ref[...] = carry_ref[...]


def _moe_route(x, modg, g_pre, router, *, tm=512):
    n, d = x.shape
    rw_hi, rw_lo, rb, n_experts = router
    tm = _tile(n, tm, GROUP * SUBLANES)
    groups = tm // GROUP
    const = lambda i: (0, 0)
    return pl.pallas_call(
        functools.partial(_route_kernel, groups=groups, n_experts=n_experts),
        grid=(n // tm,),
        in_specs=[pl.BlockSpec((tm, d), lambda i: (i, 0)),
                  pl.BlockSpec((groups, d), lambda i: (i, 3)),
                  pl.BlockSpec((groups, d), lambda i: (i, 4)),
                  pl.BlockSpec((1, d), const),
                  pl.BlockSpec((d, LANES), const),
                  pl.BlockSpec((d, LANES), const),
                  pl.BlockSpec((1, LANES), const)],
        out_specs=[pl.BlockSpec((tm, d), lambda i: (i, 0)),
                   pl.BlockSpec((tm, LANES), lambda i: (i, 0)),
                   pl.BlockSpec((1, LANES), const)],
        out_shape=[jax.ShapeDtypeStruct((n, d), F32),
                   jax.ShapeDtypeStruct((n, LANES), F32),
                   jax.ShapeDtypeStruct((1, LANES), F32)],
        scratch_shapes=[pltpu.VMEM((tm, LANES), BF16), pltpu.VMEM((1, LANES), F32)],
        compiler_params=_params(("arbitrary",)),
        name="moe_route",
    )(x, modg, modg, g_pre.reshape(1, d), rw_hi, rw_lo, rb)


def _dispatch_kernel(dest_ref, h_ref, init_ref, hs_ref, sem, *, n_tokens):
    del init_ref
    tb = h_ref.shape[0]
    base = pl.program_id(0) * tb

    def row_copy(r, k):
        slot = dest_ref[k * n_tokens + base + r]
        return pltpu.make_async_copy(h_ref.at[pl.ds(r, 1), :], hs_ref.at[pl.ds(slot, 1), :], sem)

    def start(r, carry):
        for k in range(TOP_K):
            row_copy(r, k).start()
        return carry

    def wait(r, carry):
        for k in range(TOP_K):
            row_copy(r, k).wait()
        return carry

    lax.fori_loop(0, tb, start, 0, unroll=DMA_LOOP_UNROLL)
    lax.fori_loop(0, tb, wait, 0, unroll=DMA_LOOP_UNROLL)


def _moe_dispatch(h, dest, n_rows, *, tb=512):
    n, d = h.shape
    tb = _tile(n, tb, SUBLANES)
    grid_spec = pltpu.PrefetchScalarGridSpec(
        num_scalar_prefetch=1,
        grid=(n // tb,),
        in_specs=[pl.BlockSpec((tb, d), lambda i, dest: (i, 0)),
                  pl.BlockSpec(memory_space=pl.ANY)],
        out_specs=pl.BlockSpec(memory_space=pl.ANY),
        scratch_shapes=[pltpu.SemaphoreType.DMA(())],
    )
    return pl.pallas_call(
        functools.partial(_dispatch_kernel, n_tokens=n),
        grid_spec=grid_spec,
        out_shape=jax.ShapeDtypeStruct((n_rows, d), h.dtype),
        input_output_aliases={2: 0},
        compiler_params=_params(("arbitrary",)),
        name="moe_dispatch",
    )(dest, h, jnp.zeros((n_rows, d), h.dtype))


def _experts_kernel(tile_expert_ref, n_valid_ref, hs_ref, wg_ref, wu_ref, wd_ref, o_ref, hb_ref, acc_ref):
    del tile_expert_ref
    j = pl.program_id(1)

    @pl.when(pl.program_id(0) < n_valid_ref[0])
    def _():
        @pl.when(j == 0)
        def _():
            hb_ref[...] = hs_ref[...].astype(BF16)
            acc_ref[...] = jnp.zeros_like(acc_ref)

        acc_ref[...] += _swiglu_step(hb_ref[...], wg_ref, wu_ref, wd_ref)

        @pl.when(j == pl.num_programs(1) - 1)
        def _():
            o_ref[...] = acc_ref[...]

    @pl.when(pl.program_id(0) >= n_valid_ref[0])
    def _():
        o_ref[...] = jnp.zeros_like(o_ref)


def _moe_experts(hs, tile_expert, n_valid, wg, wu, wd, *, tr, tf=1408):
    n_rows, d = hs.shape
    f = wg.shape[2]
    tf = _tile(f, tf, LANES)
    grid_spec = pltpu.PrefetchScalarGridSpec(
        num_scalar_prefetch=2,
        grid=(n_rows // tr, f // tf),
        in_specs=[pl.BlockSpec((tr, d), lambda i, j, te, nv: (i, 0)),
                  pl.BlockSpec((None, d, tf), lambda i, j, te, nv: (te[i], 0, j)),
                  pl.BlockSpec((None, d, tf), lambda i, j, te, nv: (te[i], 0, j)),
                  pl.BlockSpec((None, tf, d), lambda i, j, te, nv: (te[i], j, 0))],
        out_specs=pl.BlockSpec((tr, d), lambda i, j, te, nv: (i, 0)),
        scratch_shapes=[pltpu.VMEM((tr, d), BF16), pltpu.VMEM((tr, d), F32)],
    )
    return pl.pallas_call(
        _experts_kernel,
        grid_spec=grid_spec,
        out_shape=jax.ShapeDtypeStruct((n_rows, d), F32),
        compiler_params=_params(("arbitrary", "arbitrary")),
        name="moe_experts",
    )(tile_expert, n_valid, hs, wg, wu, wd)


def _combine_kernel(dest_ref, ys_ref, meta_ref, x_ref, gt_ref, g_ref, o_ref, buf_ref, sem, *, n_tokens, groups):
    tb = x_ref.shape[0]
    base = pl.program_id(0) * tb

    def row_copy(r, k):
        slot = dest_ref[k * n_tokens + base + r]
        return pltpu.make_async_copy(ys_ref.at[pl.ds(slot, 1), :], buf_ref.at[k, pl.ds(r, 1), :], sem)

    def start(r, carry):
        for k in range(TOP_K):
            row_copy(r, k).start()
        return carry

    def wait(r, carry):
        for k in range(TOP_K):
            row_copy(r, k).wait()
        return carry

    lax.fori_loop(0, tb, start, 0, unroll=DMA_LOOP_UNROLL)
    lax.fori_loop(0, tb, wait, 0, unroll=DMA_LOOP_UNROLL)

    meta = meta_ref[...]
    w1 = _lane_col(meta, META_W1)
    w2 = _lane_col(meta, META_W2)
    g = g_ref[...]
    for gi in range(groups):
        lo, hi = gi * GROUP, (gi + 1) * GROUP
        y = w1[lo:hi] * buf_ref[0, lo:hi, :] + w2[lo:hi] * buf_ref[1, lo:hi, :]
        o_ref[lo:hi, :] = x_ref[lo:hi, :] + gt_ref[gi:gi + 1, :] * _rms(y, g)


def _moe_combine(ys, dest, meta, x, modg, g_post, *, tb=512):
    n, d = x.shape
    tb = _tile(n, tb, GROUP * SUBLANES)
    groups = tb // GROUP
    grid_spec = pltpu.PrefetchScalarGridSpec(
        num_scalar_prefetch=1,
        grid=(n // tb,),
        in_specs=[pl.BlockSpec(memory_space=pl.ANY),
                  pl.BlockSpec((tb, LANES), lambda i, dest: (i, 0)),
                  pl.BlockSpec((tb, d), lambda i, dest: (i, 0)),
                  pl.BlockSpec((groups, d), lambda i, dest: (i, 5)),
                  pl.BlockSpec((1, d), lambda i, dest: (0, 0))],
        out_specs=pl.BlockSpec((tb, d), lambda i, dest: (i, 0)),
        scratch_shapes=[pltpu.VMEM((TOP_K, tb, d), F32), pltpu.SemaphoreType.DMA(())],
    )
    return pl.pallas_call(
        functools.partial(_combine_kernel, n_tokens=n, groups=groups),
        grid_spec=grid_spec,
        out_shape=jax.ShapeDtypeStruct((n, d), F32),
        compiler_params=_params(("arbitrary",)),
        name="moe_combine",
    )(dest, ys, meta, x, modg, g_post.reshape(1, d))


def _moe(x, modg, g_pre, g_post, router, wg, wu, wd, *, tr=512):
    n, d = x.shape
    n_experts = router[3]
    h, meta, counts = _moe_route(x, modg, g_pre, router)

    cnt = counts[0, :n_experts].astype(jnp.int32)
    padded = (cnt + tr - 1) // tr * tr
    ends = jnp.cumsum(padded)
    starts = ends - padded
    n_tiles = -(-TOP_K * n // tr) + n_experts
    experts = jnp.arange(n_experts, dtype=jnp.int32)

    def slot(i_lane, r_lane):
        idx = meta[:, i_lane].astype(jnp.int32)
        seg = jnp.sum(jnp.where(idx[:, None] == experts[None, :], starts[None, :], 0), axis=1)
        return seg + meta[:, r_lane].astype(jnp.int32)

    dest = jnp.concatenate([slot(META_I1, META_R1), slot(META_I2, META_R2)])
    tile_start = jnp.arange(n_tiles, dtype=jnp.int32) * tr
    tile_expert = jnp.minimum(jnp.sum((tile_start[:, None] >= ends[None, :]).astype(jnp.int32), axis=1), n_experts - 1)
    n_valid = (ends[-1] // tr).reshape(1)

    hs = _moe_dispatch(h, dest, n_tiles * tr)
    ys = _moe_experts(hs, tile_expert, n_valid, wg, wu, wd, tr=tr)
    return _moe_combine(ys, dest, meta, x, modg, g_post)


def _unit_lower_inverse_pair(low, eye):
    c = low.shape[0]
    n_blk = c // SUBLANES
    lane = lax.broadcasted_iota(jnp.int32, (SUBLANES, 2 * c), 1)
    second = lane >= c
    low_blk = [low[r * SUBLANES:(r + 1) * SUBLANES, :] for r in range(n_blk)]
    inv_blk = [eye[r * SUBLANES:(r + 1) * SUBLANES, :] for r in range(n_blk)]
    for j in range(c - 1):
        rj = j // SUBLANES
        pivot_row = inv_blk[rj][j % SUBLANES:j % SUBLANES + 1, :]
        col_j = jnp.where(second, c + j, j)
        for r in range(rj, n_blk):
            inv_blk[r] = inv_blk[r] - jnp.take_along_axis(low_blk[r], col_j, axis=1) * pivot_row
    return jnp.concatenate(inv_blk, axis=0)


def _gdn_kernel(seq_ref, first_ref, last_ref,
                proj_ref, ba_ref, ctx_ref, cw_ref, alog_ref, dtb_ref, gain_ref, s0_ref,
                og_ref, sout_ref, xs_ref, st_ref, *, heads, dk, dv):
    del seq_ref
    i = pl.program_id(0)
    c = GROUP
    pad = SUBLANES
    qk_w = heads * dk
    conv_dim = 2 * qk_w + heads * dv

    @pl.when(first_ref[i] == 1)
    def _():
        xs_ref[0:pad, :] = ctx_ref[...]
        st_ref[...] = s0_ref[...]

    @pl.when(first_ref[i] == 0)
    def _():
        xs_ref[0:pad, :] = xs_ref[c:c + pad, :]

    xs_ref[pad:pad + c, :] = proj_ref[:, 0:conv_dim]

    row = lax.broadcasted_iota(jnp.int32, (c, 2 * c), 0)
    lane = lax.broadcasted_iota(jnp.int32, (c, 2 * c), 1)
    is_a = lane < c
    col = jnp.where(is_a, lane, lane - c)
    strict = row > col
    causal = row >= col
    eye = jnp.where(row == col, 1.0, 0.0)
    ba = ba_ref[...]
    gain = gain_ref[...]

    def cols(c0):
        return pl.ds(pl.multiple_of(c0, LANES), LANES)

    def conv(c0):
        acc = None
        for t in range(CONV_W):
            start = pad - (CONV_W - 1) + t
            term = xs_ref[start:start + c, cols(c0)] * cw_ref[t:t + 1, cols(c0)]
            acc = term if acc is None else acc + term
        return _silu(acc)

    def block_diag(m):
        return jnp.concatenate([jnp.where(is_a, m, 0.0), jnp.where(is_a, 0.0, m)], axis=0)

    def pair_body(pair, carry):
        q, k, kb, vb, beta, g = [], [], [], [], [], []
        for hh in (2 * pair, 2 * pair + 1):
            q_h = conv(hh * dk)
            k_h = conv(qk_w + hh * dk)
            v_h = conv(2 * qk_w + hh * dv)
            q.append(q_h * lax.rsqrt(jnp.sum(q_h * q_h, axis=-1, keepdims=True) + L2_EPS) * (dk ** -0.5))
            k.append(k_h * lax.rsqrt(jnp.sum(k_h * k_h, axis=-1, keepdims=True) + L2_EPS))
            beta.append(_sigmoid(_lane_col(ba, hh)))
            g.append(-jnp.exp(_lane_col(alog_ref[...], hh)) * _softplus(_lane_col(ba, heads + hh)
                                                                         + _lane_col(dtb_ref[...], hh)))
            kb.append(k[-1] * beta[-1])
            vb.append(v_h * beta[-1])
        g_pair = jnp.where(is_a, g[0], g[1])
        gc_col = [jnp.sum(jnp.where(causal[:, :c], jnp.sum(jnp.where(eye[:, :c] > 0, g_h, 0.0), axis=0, keepdims=True),
                                    0.0), axis=1, keepdims=True) for g_h in g]
        gc_row = jnp.sum(jnp.where(row <= col, g_pair, 0.0), axis=0, keepdims=True)
        decay = jnp.exp(jnp.where(causal, jnp.where(is_a, gc_col[0], gc_col[1]) - gc_row, 0.0))
        gram = _bdot_nt(jnp.concatenate(kb + q, axis=0), jnp.concatenate(k, axis=0))
        lower = jnp.where(strict, jnp.where(is_a, gram[:c], gram[c:2 * c]) * decay, 0.0)
        a_qk = jnp.where(causal, jnp.where(is_a, gram[2 * c:3 * c], gram[3 * c:]) * decay, 0.0)
        inv = _unit_lower_inverse_pair(lower, eye)
        eg = [jnp.exp(gc_h) for gc_h in gc_col]
        rhs = jnp.concatenate([jnp.concatenate([vb[h], kb[h] * eg[h]], axis=1) for h in range(2)], axis=0)
        uw = rhs + _bdot(block_diag(inv - eye), rhs)
        v_new, qs = [], []
        for h, hh in enumerate((2 * pair, 2 * pair + 1)):
            u = uw[h * c:(h + 1) * c, :dv]
            w = uw[h * c:(h + 1) * c, dv:]
            ws_qs = _bdot(jnp.concatenate([w, q[h] * eg[h]], axis=0), st_ref[hh])
            v_new.append(u - ws_qs[:c])
            qs.append(ws_qs[c:])
        av = _bdot(block_diag(a_qk), jnp.concatenate(v_new, axis=0))
        for h, hh in enumerate((2 * pair, 2 * pair + 1)):
            gc_last = gc_col[h][c - 1:c, :]
            k_tail = k[h] * jnp.exp(gc_last - gc_col[h])
            st_ref[hh] = st_ref[hh] * jnp.exp(gc_last) + _bdot_tn(k_tail, v_new[h])
            o = qs[h] + av[h * c:(h + 1) * c]
            z = proj_ref[:, cols(conv_dim + hh * dv)]
            og_ref[:, cols(hh * dv)] = (_rms(o, gain) * _silu(z)).astype(og_ref.dtype)
        return carry

    lax.fori_loop(0, heads // 2, pair_body, 0, unroll=PAIR_UNROLL)

    @pl.when(last_ref[i] == 1)
    def _():
        sout_ref[...] = st_ref[...]


def _gdn_core(proj, ba, ctx8, conv_w, a_log, dt_bias, o_gain, s0, seq_of_chunk, first, last, *, heads, dk, dv):
    n, pw = proj.shape
    conv_dim = 2 * heads * dk + heads * dv
    n_seq = s0.shape[0]
    c = GROUP
    assert n % c == 0 and dk == LANES and dv == LANES and 2 * c == LANES and heads % 2 == 0

    def lane_row(vec):
        return jnp.zeros((1, LANES), F32).at[0, :vec.shape[0]].set(vec.astype(F32))

    grid_spec = pltpu.PrefetchScalarGridSpec(
        num_scalar_prefetch=3,
        grid=(n // c,),
        in_specs=[pl.BlockSpec((c, pw), lambda i, s, f, l: (i, 0)),
                  pl.BlockSpec((c, LANES), lambda i, s, f, l: (i, 0)),
                  pl.BlockSpec((None, SUBLANES, conv_dim), lambda i, s, f, l: (s[i], 0, 0)),
                  pl.BlockSpec((CONV_W, conv_dim), lambda i, s, f, l: (0, 0)),
                  pl.BlockSpec((1, LANES), lambda i, s, f, l: (0, 0)),
                  pl.BlockSpec((1, LANES), lambda i, s, f, l: (0, 0)),
                  pl.BlockSpec((1, dv), lambda i, s, f, l: (0, 0)),
                  pl.BlockSpec((None, heads, dk, dv), lambda i, s, f, l: (s[i], 0, 0, 0))],
        out_specs=[pl.BlockSpec((c, heads * dv), lambda i, s, f, l: (i, 0)),
                   pl.BlockSpec((None, heads, dk, dv), lambda i, s, f, l: (s[i], 0, 0, 0))],
        scratch_shapes=[pltpu.VMEM((c + SUBLANES, conv_dim), F32),
                        pltpu.VMEM((heads, dk, dv), F32)],
    )
    return pl.pallas_call(
        functools.partial(_gdn_kernel, heads=heads, dk=dk, dv=dv),
        grid_spec=grid_spec,
        out_shape=[jax.ShapeDtypeStruct((n, heads * dv), BF16),
                   jax.ShapeDtypeStruct((n_seq, heads, dk, dv), F32)],
        compiler_params=_params(("arbitrary",)),
        name="gdn_core",
    )(seq_of_chunk, first, last, proj, ba, ctx8, conv_w, lane_row(a_log), lane_row(dt_bias),
      o_gain.reshape(1, dv).astype(F32), s0)


def _cumsum_kernel(x_ref, o_ref):
    heads, t = x_ref.shape
    upper = (lax.broadcasted_iota(jnp.int32, (LANES, LANES), 0)
             <= lax.broadcasted_iota(jnp.int32, (LANES, LANES), 1)).astype(BF16)

    def body(i, carry):
        cols = pl.ds(pl.multiple_of(i * LANES, LANES), LANES)
        x = x_ref[:, cols]
        x1 = x.astype(BF16)
        r1 = x - x1.astype(F32)
        x2 = r1.astype(BF16)
        x3 = (r1 - x2.astype(F32)).astype(BF16)
        dot = functools.partial(jnp.dot, preferred_element_type=F32)
        out = carry + (dot(x1, upper) + (dot(x2, upper) + dot(x3, upper)))
        o_ref[:, cols] = out
        return out[:, LANES - 1:LANES]

    lax.fori_loop(0, t // LANES, body, jnp.zeros((heads, 1), F32))


def _cumsum_rows(x):
    b, heads, t = x.shape
    assert t % LANES == 0
    return pl.pallas_call(
        _cumsum_kernel,
        grid=(b,),
        in_specs=[pl.BlockSpec((None, heads, t), lambda i: (i, 0, 0))],
        out_specs=pl.BlockSpec((None, heads, t), lambda i: (i, 0, 0)),
        out_shape=jax.ShapeDtypeStruct((b, heads, t), F32),
        compiler_params=_params(("arbitrary",)),
        name="logf_cumsum",
    )(x)


def _fox_kernel(q_ref, kn_ref, vn_ref, *rest, tq, tc, n_cache, hd):
    if n_cache:
        kc_ref, vc_ref, f_ref, o_ref, kb_ref, vb_ref, *s_bufs = rest
        assert kn_ref.shape[0] == tq
    else:
        f_ref, o_ref, kb_ref, vb_ref, *s_bufs = rest
    qi = pl.program_id(2)
    q_off = n_cache * tc

    @pl.when(qi == 0)
    def _():
        kb_ref[...] = kn_ref[...].astype(BF16)
        vb_ref[...] = vn_ref[...].astype(BF16)

    lane = lax.broadcasted_iota(jnp.int32, (tq, LANES), 1)
    row = lax.broadcasted_iota(jnp.int32, (tq, tq), 0)
    col = lax.broadcasted_iota(jnp.int32, (tq, tq), 1)
    q2 = q_ref[...] * (hd ** -0.5)
    single_q_tile = kn_ref.shape[0] == tq
    q_start = 0 if single_q_tile else pl.multiple_of(qi * tq, tq)
    halves = range(LANES // hd)
    in_half = [(lane >= h * hd) & (lane < (h + 1) * hd) for h in halves]
    q_stack = jnp.concatenate([jnp.where(in_half[h], q2, jnp.zeros_like(q2)) for h in halves], axis=0)
    f_q_rows = [f_ref[h, :, pl.ds(q_off + q_start, tq)] for h in halves]
    f_q = [jnp.sum(jnp.where(row == col, f_q_rows[h], 0.0), axis=1, keepdims=True) for h in halves]

    def scores(k_t):
        return lax.dot_general(q_stack, k_t, (((1,), (1,)), ((), ())), preferred_element_type=F32)

    def consume(carry, s_buf, tk, v_t, f_k_rows, mask):
        stats, probs = [], []
        for h in halves:
            m, l, _ = carry[h]
            s = s_buf[h * tq:(h + 1) * tq, 0:tk] + (f_q[h] - f_k_rows[h])
            if mask is not None:
                s = jnp.where(mask, s, NEG_INF)
            m_new = jnp.maximum(m, jnp.max(s, axis=1, keepdims=True))
            alpha = jnp.exp(m - m_new)
            p = jnp.exp(s - m_new)
            stats.append((m_new, alpha * l + jnp.sum(p, axis=1, keepdims=True), alpha))
            probs.append(p.astype(BF16))
        pv = jnp.dot(jnp.concatenate(probs, axis=0), v_t, preferred_element_type=F32)
        return tuple((stats[h][0], stats[h][1], stats[h][2] * carry[h][2] + pv[h * tq:(h + 1) * tq]) for h in halves)

    carry = tuple((jnp.full((tq, 1), NEG_INF, F32), jnp.zeros((tq, 1), F32), jnp.zeros((tq, LANES), F32))
                  for _ in halves)

    if n_cache:
        for t in range(n_cache + 1):
            cur, nxt = s_bufs[t % 2], s_bufs[(t + 1) % 2]
            if t == 0:
                cur[:, 0:tc] = scores(kc_ref[0:tc, :].astype(BF16))
            if t + 1 < n_cache:
                nxt[:, 0:tc] = scores(kc_ref[(t + 1) * tc:(t + 2) * tc, :].astype(BF16))
            elif t + 1 == n_cache:
                nxt[:, 0:tq] = scores(kb_ref[...])
            if t < n_cache:
                rows = pl.ds(t * tc, tc)
                carry = consume(carry, cur, tc, vc_ref[rows, :].astype(BF16), [f_ref[h, :, rows] for h in halves], None)
            else:
                carry = consume(carry, cur, tq, vb_ref[...], f_q_rows, row >= col)
    else:
        n_q = kn_ref.shape[0] // tq

        def tile_rows(t):
            return pl.ds(pl.multiple_of(jnp.minimum(t, n_q - 1) * tq, tq), tq)

        def consume_tile(carry, s_buf, t):
            rows = tile_rows(t)
            visible = (t * tq + col) <= (q_start + row)
            return consume(carry, s_buf, tq, vb_ref[rows, :], [f_ref[h, :, rows] for h in halves], visible)

        s_bufs[0][...] = scores(kb_ref[tile_rows(0), :])

        def pair_body(i, carry):
            s_bufs[1][...] = scores(kb_ref[tile_rows(2 * i + 1), :])
            carry = consume_tile(carry, s_bufs[0], 2 * i)
            s_bufs[0][...] = scores(kb_ref[tile_rows(2 * i + 2), :])
            return consume_tile(carry, s_bufs[1], 2 * i + 1)
        carry = lax.fori_loop(0, (qi + 2) // 2, pair_body, carry)

    out = carry[0][2] / carry[0][1]
    for h in list(halves)[1:]:
        out = jnp.where(in_half[h], carry[h][2] / carry[h][1], out)
    o_ref[...] = out.astype(o_ref.dtype)


def _fox_attention(q, k_new, v_new, f_rows, row0, n_batch, t_new, cache=None, *, heads, hd, tq, tc=512):
    width = heads * hd
    per_blk = LANES // hd
    assert t_new % tq == 0 and row0 % tq == 0
    assert tq % LANES == 0 or (tq == t_new and cache is not None)
    n_q = t_new // tq
    n_cache = 0
    in_specs = [pl.BlockSpec((tq, LANES), lambda b, h, i: (row0 // tq + b * n_q + i, h)),
                pl.BlockSpec((t_new, LANES), lambda b, h, i: (b, h)),
                pl.BlockSpec((t_new, LANES), lambda b, h, i: (b, h))]
    args = [q, k_new, v_new]
    if cache is not None:
        t_cache = cache[0].shape[1]
        tc = _tile(t_cache, tc, LANES)
        n_cache = t_cache // tc
        in_specs += [pl.BlockSpec((None, t_cache, LANES), lambda b, h, i: (b, 0, h)),
                     pl.BlockSpec((None, t_cache, LANES), lambda b, h, i: (b, 0, h))]
        args += list(cache)
    t_pad = f_rows.shape[-1]
    in_specs.append(pl.BlockSpec((None, per_blk, 1, t_pad), lambda b, h, i: (b, h, 0, 0)))
    args.append(f_rows)
    return pl.pallas_call(
        functools.partial(_fox_kernel, tq=tq, tc=tc, n_cache=n_cache, hd=hd),
        grid=(n_batch, width // LANES, n_q),
        in_specs=in_specs,
        out_specs=pl.BlockSpec((tq, LANES), lambda b, h, i: (b * n_q + i, h)),
        out_shape=jax.ShapeDtypeStruct((n_batch * t_new, width), BF16),
        scratch_shapes=[pltpu.VMEM((t_new, LANES), BF16), pltpu.VMEM((t_new, LANES), BF16)]
        + [pltpu.VMEM((per_blk * tq, max(tq, tc) if cache is not None else tq), F32)] * 2,
        compiler_params=_params(("arbitrary", "arbitrary", "arbitrary")),
        name="fox_cache" if cache is not None else "fox_prompt",
    )(*args)


def kernel(x_prompt, x_sample, c_prompt, c_sample, cache_k, cache_v, cache_logf, state_delta, state_conv, ada_w, ada_b, pre_mix_g, post_mix_g, pre_ffn_g, post_ffn_g, gdn_w_in, gdn_conv_w, gdn_a_log, gdn_dt_bias, gdn_o_gain, gdn_w_out, kv_norm_g, kv_ada_w, kv_ada_b, kv_w, kv_b_f, fox_w_q, fox_w_o, ffn_w_gate, ffn_w_up, ffn_w_down, moe_router_w, moe_router_b, moe_w_gate, moe_w_up, moe_w_down):
    bp, tp, d = x_prompt.shape
    bs, ts, _ = x_sample.shape
    n_p, n_s = bp * tp, bs * ts
    n = n_p + n_s
    depth = ada_w.shape[0]
    n_gdn = gdn_w_in.shape[0]
    gdn_heads, gdn_dk, gdn_dv = state_delta.shape[2:]
    qk_w, v_w = gdn_heads * gdn_dk, gdn_heads * gdn_dv
    conv_dim = 2 * qk_w + v_w
    fox_heads, fox_hd = cache_k.shape[2:]
    fox_w = fox_heads * fox_hd
    t_cache = cache_k.shape[1]
    n_experts = moe_router_w.shape[-1]
    assert tp % GROUP == 0 and ts % GROUP == 0

    x = jnp.concatenate([x_prompt.reshape(n_p, d), x_sample.reshape(n_s, d)], axis=0)
    c = jnp.concatenate([c_prompt, c_sample], axis=0)

    group_batch = np.concatenate([np.repeat(np.arange(bp), tp // GROUP),
                                  bp + np.repeat(np.arange(bs), ts // GROUP)])

    def per_group(m):
        n_l, _, width = m.shape
        rep = lambda a, b, t: jnp.broadcast_to(a[:, :, None, :], (n_l, b, t // GROUP, width)).reshape(n_l, -1, width)
        return jnp.concatenate([rep(m[:, :bp], bp, tp), rep(m[:, bp:], bs, ts)], axis=1)

    modg = per_group(_mod_matmul(c, ada_w, ada_b))
    kv_modg = per_group(_mod_matmul(c, kv_ada_w[None], kv_ada_b[None]))[0]

    seq_of_chunk = jnp.asarray(group_batch, jnp.int32)
    chunk_in_seq = np.concatenate([np.tile(np.arange(tp // GROUP), bp), np.tile(np.arange(ts // GROUP), bs)])
    seq_chunks = np.concatenate([np.full(n_p // GROUP, tp // GROUP), np.full(n_s // GROUP, ts // GROUP)])
    first = jnp.asarray(chunk_in_seq == 0, jnp.int32)
    last = jnp.asarray(chunk_in_seq == seq_chunks - 1, jnp.int32)

    cache_kv = tuple(a.reshape(bs, t_cache, fox_w).astype(BF16) for a in (cache_k, cache_v))
    delta_out, conv_out = [], []
    kv_p = kv_s = logf = f_p = f_s = None
    for l in range(depth):
        g_mix = pre_mix_g[l]
        if l < n_gdn:
            w_in = gdn_w_in[l]
            w_qkvz = w_in[:, :conv_dim + v_w].astype(BF16)
            w_ba = jnp.zeros((d, LANES), BF16).at[:, :2 * gdn_heads].set(w_in[:, conv_dim + v_w:].astype(BF16))
            proj = _norm_matmul(x, modg[l], 0, 1, g_mix, w_qkvz, out_dtype=F32, name="gdn_in_proj")
            ba = _norm_matmul(x, modg[l], 0, 1, g_mix, w_ba, out_dtype=F32, name="gdn_gate_proj")
            ctx = jnp.concatenate([jnp.zeros((bp, CONV_W - 1, conv_dim), F32), state_conv[:, l]], axis=0)
            ctx8 = jnp.pad(ctx, ((0, 0), (SUBLANES - (CONV_W - 1), 0), (0, 0)))
            s0 = jnp.concatenate([jnp.zeros((bp,) + state_delta.shape[2:], F32), state_delta[:, l]], axis=0)
            mixed, s_out = _gdn_core(proj, ba, ctx8, gdn_conv_w[l], gdn_a_log[l], gdn_dt_bias[l], gdn_o_gain[l],
                                     s0, seq_of_chunk, first, last, heads=gdn_heads, dk=gdn_dk, dv=gdn_dv)
            delta_out.append(s_out)
            tail = lambda a, b, t: a.reshape(b, t, -1)[:, t - (CONV_W - 1):, :conv_dim]
            conv_out.append(jnp.concatenate([tail(proj[:n_p], bp, tp), tail(proj[n_p:], bs, ts)], axis=0))
            w_o = gdn_w_out[l].astype(BF16)
        else:
            j = l - n_gdn
            q = _norm_matmul(x, modg[l], 0, 1, g_mix, fox_w_q[j].astype(BF16), out_dtype=BF16, name="fox_q_proj")
            o_p = _fox_attention(q, kv_p[0], kv_p[1], f_p, 0, bp, tp, heads=fox_heads, hd=fox_hd, tq=min(tp, 256))
            o_s = _fox_attention(q, kv_s[0], kv_s[1], f_s, n_p, bs, ts, cache=cache_kv, heads=fox_heads, hd=fox_hd, tq=ts)
            mixed = jnp.concatenate([o_p, o_s], axis=0)
            w_o = fox_w_o[j].astype(BF16)
        x = _matmul_post(mixed, w_o, x, modg[l], 2, post_mix_g[l], name="mixer_out_proj")

        i = l // 2
        if l % 2 == 0:
            x = _ffn(x, modg[l], pre_ffn_g[l], post_ffn_g[l], ffn_w_gate[i].astype(BF16),
                     ffn_w_up[i].astype(BF16), ffn_w_down[i].astype(BF16), name="dense_ffn")
        else:
            rw = jnp.zeros((d, LANES), F32).at[:, :n_experts].set(moe_router_w[i])
            rw_hi = rw.astype(BF16)
            rw_lo = (rw - rw_hi.astype(F32)).astype(BF16)
            rb = jnp.zeros((1, LANES), F32).at[0, :n_experts].set(moe_router_b[i])
            x = _moe(x, modg[l], pre_ffn_g[l], post_ffn_g[l], (rw_hi, rw_lo, rb, n_experts),
                     moe_w_gate[i].astype(BF16), moe_w_up[i].astype(BF16), moe_w_down[i].astype(BF16))

        if l == n_gdn - 1:
            w_k = kv_w[:, :fox_w].astype(BF16)
            w_v = kv_w[:, fox_w:2 * fox_w].astype(BF16)
            kv_p, kv_s = [[_norm_matmul(x, kv_modg, 0, 1, kv_norm_g, w, out_dtype=F32, rows=rows, name=name)
                           for w, name in ((w_k, "kv_k_proj"), (w_v, "kv_v_proj"))]
                          for rows in ((0, n_p), (n_p, n_s))]
            w_f = jnp.zeros((d, LANES), BF16).at[:, :fox_heads].set(kv_w[:, 2 * fox_w:].astype(BF16))
            b_f = jnp.zeros((LANES,), F32).at[:fox_heads].set(kv_b_f)
            logf = _norm_matmul(x, kv_modg, 0, 1, kv_norm_g, w_f, out_dtype=F32, bias=b_f, name="kv_f_proj")[:, :fox_heads]
            logf_p = logf[:n_p].reshape(bp, tp, fox_heads)
            logf_s = logf[n_p:].reshape(bs, ts, fox_heads)
            f_p = _cumsum_rows(jnp.swapaxes(logf_p, 1, 2))[:, :, None, :]
            rows_s = jnp.swapaxes(jnp.concatenate([cache_logf, logf_s], axis=1), 1, 2)
            t_all = t_cache + ts
            t_pad = -(-t_all // LANES) * LANES
            f_s = _cumsum_rows(jnp.pad(rows_s, ((0, 0), (0, 0), (0, t_pad - t_all))))[:, :, None, :]

    delta = jnp.stack(delta_out, axis=1)
    conv = jnp.stack(conv_out, axis=1)
    return (x[:n_p].reshape(bp, tp, d), x[n_p:].reshape(bs, ts, d),
            kv_p[0].reshape(bp, tp, fox_heads, fox_hd), kv_p[1].reshape(bp, tp, fox_heads, fox_hd),
            logf[:n_p].reshape(bp, tp, fox_heads), delta[:bp], conv[:bp],
            kv_s[0].reshape(bs, ts, fox_heads, fox_hd), kv_s[1].reshape(bs, ts, fox_heads, fox_hd),
            logf[n_p:].reshape(bs, ts, fox_heads), delta[bp:], conv[bp:])
```

```python
import functools
import math

import jax
import jax.numpy as jnp
import numpy as np
from jax import lax
from jax.experimental import pallas as pl
from jax.experimental.pallas import tpu as pltpu

F32 = jnp.float32
BF16 = jnp.bfloat16
NORM_EPS = 1e-6
L2_EPS = 1e-6
GROUP = 64
LANES = 128
SUBLANES = 8
VMEM_LIMIT_BYTES = 56 * 1024 * 1024
CONV_W = 4
TOP_K = 2
NEG_INF = float("-inf")
DMA_LOOP_UNROLL = 8


def _sigmoid(x):
    return 1.0 / (1.0 + jnp.exp(-x))


def _silu(x):
    return x * _sigmoid(x)


def _softplus(x):
    return jnp.maximum(x, 0.0) + jnp.log1p(jnp.exp(-jnp.abs(x)))


def _rms(x, g):
    return x * lax.rsqrt(jnp.mean(x * x, axis=-1, keepdims=True) + NORM_EPS) * g


def _bdot(a, b):
    return jnp.dot(a.astype(BF16), b.astype(BF16), preferred_element_type=F32)


def _bdot_nt(a, b):
    return lax.dot_general(a.astype(BF16), b.astype(BF16), (((1,), (1,)), ((), ())),
                           preferred_element_type=F32)


def _bdot_tn(a, b):
    return lax.dot_general(a.astype(BF16), b.astype(BF16), (((0,), (0,)), ((), ())),
                           preferred_element_type=F32)


def _split2(x):
    hi = x.astype(BF16)
    lo = (x - hi.astype(F32)).astype(BF16)
    return hi, lo


def _dot_split(a, b):
    a_hi, a_lo = _split2(a)
    b_hi, b_lo = _split2(b)
    dot = functools.partial(jnp.dot, preferred_element_type=F32)
    return dot(a_hi, b_hi) + (dot(a_hi, b_lo) + dot(a_lo, b_hi))


def _lane_col(x, idx):
    lane = lax.broadcasted_iota(jnp.int32, x.shape, 1)
    return jnp.sum(jnp.where(lane == idx, x, 0.0), axis=1, keepdims=True)


def _modulated_norm(x_ref, sh_ref, sc_ref, g_ref, groups, emit):
    g = g_ref[...]
    for gi in range(groups):
        rows = pl.ds(gi * GROUP, GROUP)
        h = _rms(x_ref[rows, :], g) * (1.0 + sc_ref[gi:gi + 1, :]) + sh_ref[gi:gi + 1, :]
        emit(rows, h)


def _gated_residual(y_ref, x_ref, gt_ref, g_ref, o_ref, groups):
    g = g_ref[...]
    for gi in range(groups):
        rows = pl.ds(gi * GROUP, GROUP)
        o_ref[rows, :] = x_ref[rows, :] + gt_ref[gi:gi + 1, :] * _rms(y_ref[rows, :], g)


def _tile(n, cap, unit):
    best = None
    for t in range(unit, min(n, cap) + 1, unit):
        if n % t == 0:
            best = t
    assert best is not None, (n, cap, unit)
    return best


def _params(semantics):
    return pltpu.CompilerParams(dimension_semantics=semantics, vmem_limit_bytes=VMEM_LIMIT_BYTES)


def _mod_kernel(c_ref, w_ref, b_ref, o_ref):
    s = _silu(c_ref[...])
    o_ref[...] = _bdot(s, w_ref[...]) + b_ref[...]


def _mod_matmul(c, w, b, tn=1024):
    n_l, d, width = w.shape
    m = c.shape[0]
    tn = _tile(width, tn, LANES)
    return pl.pallas_call(
        _mod_kernel,
        grid=(n_l, width // tn),
        in_specs=[pl.BlockSpec((m, d), lambda l, j: (0, 0)),
                  pl.BlockSpec((None, d, tn), lambda l, j: (l, 0, j)),
                  pl.BlockSpec((None, 1, tn), lambda l, j: (l, 0, j))],
        out_specs=pl.BlockSpec((None, m, tn), lambda l, j: (l, 0, j)),
        out_shape=jax.ShapeDtypeStruct((n_l, m, width), F32),
        compiler_params=_params(("arbitrary", "arbitrary")),
        name="ada_mod",
    )(c, w, b.reshape(n_l, 1, width))


def _norm_mm_kernel(x_ref, sh_ref, sc_ref, g_ref, w_ref, *rest, groups, log_sigmoid):
    if log_sigmoid:
        b_ref, o_ref, h_ref = rest
    else:
        o_ref, h_ref = rest

    @pl.when(pl.program_id(1) == 0)
    def _():
        def emit(rows, h):
            h_ref[rows, :] = h.astype(BF16)
        _modulated_norm(x_ref, sh_ref, sc_ref, g_ref, groups, emit)

    y = jnp.dot(h_ref[...], w_ref[...], preferred_element_type=F32)
    if log_sigmoid:
        y = -_softplus(-(y + b_ref[...]))
    o_ref[...] = y.astype(o_ref.dtype)


def _norm_matmul(x, modg, sh_col, sc_col, g, w, *, out_dtype, bias=None, rows=None, tm=1024, tn=2048, name):
    d = x.shape[1]
    row0, n = rows if rows is not None else (0, x.shape[0])
    width = w.shape[1]
    tm = _tile(math.gcd(n, row0) if row0 else n, tm, GROUP * SUBLANES)
    tn = _tile(width, tn, LANES)
    groups = tm // GROUP
    t0 = row0 // tm
    in_specs = [pl.BlockSpec((tm, d), lambda i, j: (t0 + i, 0)),
                pl.BlockSpec((groups, d), lambda i, j: (t0 + i, sh_col)),
                pl.BlockSpec((groups, d), lambda i, j: (t0 + i, sc_col)),
                pl.BlockSpec((1, d), lambda i, j: (0, 0)),
                pl.BlockSpec((d, tn), lambda i, j: (0, j))]
    args = [x, modg, modg, g.reshape(1, d), w]
    if bias is not None:
        in_specs.append(pl.BlockSpec((1, tn), lambda i, j: (0, j)))
        args.append(bias.reshape(1, width))
    return pl.pallas_call(
        functools.partial(_norm_mm_kernel, groups=groups, log_sigmoid=bias is not None),
        grid=(n // tm, width // tn),
        in_specs=in_specs,
        out_specs=pl.BlockSpec((tm, tn), lambda i, j: (i, j)),
        out_shape=jax.ShapeDtypeStruct((n, width), out_dtype),
        scratch_shapes=[pltpu.VMEM((tm, d), BF16)],
        compiler_params=_params(("arbitrary", "arbitrary")),
        name=name,
    )(*args)


def _mm_post_kernel(a_ref, w_ref, x_ref, gt_ref, g_ref, o_ref, y_ref, *, groups):
    y_ref[...] = jnp.dot(a_ref[...], w_ref[...], preferred_element_type=F32)
    _gated_residual(y_ref, x_ref, gt_ref, g_ref, o_ref, groups)


def _matmul_post(a, w, x, modg, gt_col, g, *, tm=512, name):
    n, d = x.shape
    k = a.shape[1]
    tm = _tile(n, tm, GROUP * SUBLANES)
    groups = tm // GROUP
    return pl.pallas_call(
        functools.partial(_mm_post_kernel, groups=groups),
        grid=(n // tm,),
        in_specs=[pl.BlockSpec((tm, k), lambda i: (i, 0)),
                  pl.BlockSpec((k, d), lambda i: (0, 0)),
                  pl.BlockSpec((tm, d), lambda i: (i, 0)),
                  pl.BlockSpec((groups, d), lambda i: (i, gt_col)),
                  pl.BlockSpec((1, d), lambda i: (0, 0))],
        out_specs=pl.BlockSpec((tm, d), lambda i: (i, 0)),
        out_shape=jax.ShapeDtypeStruct((n, d), F32),
        scratch_shapes=[pltpu.VMEM((tm, d), F32)],
        compiler_params=_params(("arbitrary",)),
        name=name,
    )(a, w, x, modg, g.reshape(1, d))


def _swiglu_step(h, wg_ref, wu_ref, wd_ref):
    gate = jnp.dot(h, wg_ref[...], preferred_element_type=F32)
    up = jnp.dot(h, wu_ref[...], preferred_element_type=F32)
    act = (_silu(gate) * up).astype(BF16)
    return jnp.dot(act, wd_ref[...], preferred_element_type=F32)


def _ffn_kernel(x_ref, sh_ref, sc_ref, gt_ref, gpre_ref, gpost_ref, wg_ref, wu_ref, wd_ref, o_ref, h_ref, acc_ref,
                *, groups):
    j = pl.program_id(1)

    @pl.when(j == 0)
    def _():
        def emit(rows, h):
            h_ref[rows, :] = h.astype(BF16)
        _modulated_norm(x_ref, sh_ref, sc_ref, gpre_ref, groups, emit)
        acc_ref[...] = jnp.zeros_like(acc_ref)

    acc_ref[...] += _swiglu_step(h_ref[...], wg_ref, wu_ref, wd_ref)

    @pl.when(j == pl.num_programs(1) - 1)
    def _():
        _gated_residual(acc_ref, x_ref, gt_ref, gpost_ref, o_ref, groups)


def _ffn(x, modg, g_pre, g_post, wg, wu, wd, *, tm=512, tf=1408, name):
    n, d = x.shape
    f = wg.shape[1]
    tm = _tile(n, tm, GROUP * SUBLANES)
    tf = _tile(f, tf, LANES)
    groups = tm // GROUP
    row = lambda i, j: (i, 0)
    return pl.pallas_call(
        functools.partial(_ffn_kernel, groups=groups),
        grid=(n // tm, f // tf),
        in_specs=[pl.BlockSpec((tm, d), row),
                  pl.BlockSpec((groups, d), lambda i, j: (i, 3)),
                  pl.BlockSpec((groups, d), lambda i, j: (i, 4)),
                  pl.BlockSpec((groups, d), lambda i, j: (i, 5)),
                  pl.BlockSpec((1, d), lambda i, j: (0, 0)),
                  pl.BlockSpec((1, d), lambda i, j: (0, 0)),
                  pl.BlockSpec((d, tf), lambda i, j: (0, j)),
                  pl.BlockSpec((d, tf), lambda i, j: (0, j)),
                  pl.BlockSpec((tf, d), lambda i, j: (j, 0))],
        out_specs=pl.BlockSpec((tm, d), row),
        out_shape=jax.ShapeDtypeStruct((n, d), F32),
        scratch_shapes=[pltpu.VMEM((tm, d), BF16), pltpu.VMEM((tm, d), F32)],
        compiler_params=_params(("arbitrary", "arbitrary")),
        name=name,
    )(x, modg, modg, modg, g_pre.reshape(1, d), g_post.reshape(1, d), wg, wu, wd)


META_I1, META_I2, META_R1, META_R2, META_W1, META_W2 = range(6)


def _top2(logits, n_experts):
    lane = lax.broadcasted_iota(jnp.int32, logits.shape, 1).astype(F32)
    big = float(LANES)
    logits = jnp.where(lane < n_experts, logits, NEG_INF)
    m1 = jnp.max(logits, axis=1, keepdims=True)
    i1 = jnp.min(jnp.where(logits == m1, lane, big), axis=1, keepdims=True)
    rest = jnp.where(lane == i1, NEG_INF, logits)
    m2 = jnp.max(rest, axis=1, keepdims=True)
    i2 = jnp.min(jnp.where(rest == m2, lane, big), axis=1, keepdims=True)
    e2 = jnp.exp(m2 - m1)
    den = 1.0 + e2
    return lane, i1, i2, 1.0 / den, e2 / den


def _route_kernel(x_ref, sh_ref, sc_ref, g_ref, rw_hi_ref, rw_lo_ref, rb_ref,
                  h_ref, meta_ref, cnt_ref, sel_ref, carry_ref, *, groups, n_experts):
    tm = x_ref.shape[0]

    @pl.when(pl.program_id(0) == 0)
    def _():
        carry_ref[...] = jnp.zeros_like(carry_ref)

    def emit(rows, h):
        h_ref[rows, :] = h
        h_hi, h_lo = _split2(h)
        dot = functools.partial(jnp.dot, preferred_element_type=F32)
        logits = dot(h_hi, rw_hi_ref[...]) + (dot(h_hi, rw_lo_ref[...]) + dot(h_lo, rw_hi_ref[...]))
        lane, i1, i2, w1, w2 = _top2(logits + rb_ref[...], n_experts)
        sel_ref[rows, :] = jnp.where((lane == i1) | (lane == i2), 1.0, 0.0).astype(BF16)
        meta_ref[rows, :] = (jnp.where(lane == META_I1, i1, 0.0) + jnp.where(lane == META_I2, i2, 0.0)
                             + jnp.where(lane == META_W1, w1, 0.0) + jnp.where(lane == META_W2, w2, 0.0))
    _modulated_norm(x_ref, sh_ref, sc_ref, g_ref, groups, emit)

    sel = sel_ref[...]
    earlier = (lax.broadcasted_iota(jnp.int32, (tm, tm), 0) > lax.broadcasted_iota(jnp.int32, (tm, tm), 1))
    rank = jnp.dot(earlier.astype(BF16), sel, preferred_element_type=F32) + carry_ref[...]
    meta = meta_ref[...]
    lane = lax.broadcasted_iota(jnp.int32, (tm, LANES), 1).astype(F32)
    r1 = jnp.sum(jnp.where(lane == _lane_col(meta, META_I1), rank, 0.0), axis=1, keepdims=True)
    r2 = jnp.sum(jnp.where(lane == _lane_col(meta, META_I2), rank, 0.0), axis=1, keepdims=True)
    meta_ref[...] = meta + jnp.where(lane == META_R1, r1, 0.0) + jnp.where(lane == META_R2, r2, 0.0)
    carry_ref[...] += jnp.sum(sel.astype(F32), axis=0, keepdims=True)
    cnt_ref[...] = carry_ref[...]


def _moe_route(x, modg, g_pre, router, *, tm=512):
    n, d = x.shape
    rw_hi, rw_lo, rb, n_experts = router
    tm = _tile(n, tm, GROUP * SUBLANES)
    groups = tm // GROUP
    const = lambda i: (0, 0)
    return pl.pallas_call(
        functools.partial(_route_kernel, groups=groups, n_experts=n_experts),
        grid=(n // tm,),
        in_specs=[pl.BlockSpec((tm, d), lambda i: (i, 0)),
                  pl.BlockSpec((groups, d), lambda i: (i, 3)),
                  pl.BlockSpec((groups, d), lambda i: (i, 4)),
                  pl.BlockSpec((1, d), const),
                  pl.BlockSpec((d, LANES), const),
                  pl.BlockSpec((d, LANES), const),
                  pl.BlockSpec((1, LANES), const)],
        out_specs=[pl.BlockSpec((tm, d), lambda i: (i, 0)),
                   pl.BlockSpec((tm, LANES), lambda i: (i, 0)),
                   pl.BlockSpec((1, LANES), const)],
        out_shape=[jax.ShapeDtypeStruct((n, d), F32),
                   jax.ShapeDtypeStruct((n, LANES), F32),
                   jax.ShapeDtypeStruct((1, LANES), F32)],
        scratch_shapes=[pltpu.VMEM((tm, LANES), BF16), pltpu.VMEM((1, LANES), F32)],
        compiler_params=_params(("arbitrary",)),
        name="moe_route",
    )(x, modg, modg, g_pre.reshape(1, d), rw_hi, rw_lo, rb)


def _dispatch_kernel(dest_ref, h_ref, init_ref, hs_ref, sem, *, n_tokens):
    del init_ref
    tb = h_ref.shape[0]
    base = pl.program_id(0) * tb

    def row_copy(r, k):
        slot = dest_ref[k * n_tokens + base + r]
        return pltpu.make_async_copy(h_ref.at[pl.ds(r, 1), :], hs_ref.at[pl.ds(slot, 1), :], sem)

    def start(r, carry):
        for k in range(TOP_K):
            row_copy(r, k).start()
        return carry

    def wait(r, carry):
        for k in range(TOP_K):
            row_copy(r, k).wait()
        return carry

    lax.fori_loop(0, tb, start, 0, unroll=DMA_LOOP_UNROLL)
    lax.fori_loop(0, tb, wait, 0, unroll=DMA_LOOP_UNROLL)


def _moe_dispatch(h, dest, n_rows, *, tb=512):
    n, d = h.shape
    tb = _tile(n, tb, SUBLANES)
    grid_spec = pltpu.PrefetchScalarGridSpec(
        num_scalar_prefetch=1,
        grid=(n // tb,),
        in_specs=[pl.BlockSpec((tb, d), lambda i, dest: (i, 0)),
                  pl.BlockSpec(memory_space=pl.ANY)],
        out_specs=pl.BlockSpec(memory_space=pl.ANY),
        scratch_shapes=[pltpu.SemaphoreType.DMA(())],
    )
    return pl.pallas_call(
        functools.partial(_dispatch_kernel, n_tokens=n),
        grid_spec=grid_spec,
        out_shape=jax.ShapeDtypeStruct((n_rows, d), h.dtype),
        input_output_aliases={2: 0},
        compiler_params=_params(("arbitrary",)),
        name="moe_dispatch",
    )(dest, h, jnp.zeros((n_rows, d), h.dtype))


def _experts_kernel(tile_expert_ref, n_valid_ref, hs_ref, wg_ref, wu_ref, wd_ref, o_ref, hb_ref, acc_ref):
    del tile_expert_ref
    j = pl.program_id(1)

    @pl.when(pl.program_id(0) < n_valid_ref[0])
    def _():
        @pl.when(j == 0)
        def _():
            hb_ref[...] = hs_ref[...].astype(BF16)
            acc_ref[...] = jnp.zeros_like(acc_ref)

        acc_ref[...] += _swiglu_step(hb_ref[...], wg_ref, wu_ref, wd_ref)

        @pl.when(j == pl.num_programs(1) - 1)
        def _():
            o_ref[...] = acc_ref[...]

    @pl.when(pl.program_id(0) >= n_valid_ref[0])
    def _():
        o_ref[...] = jnp.zeros_like(o_ref)


def _moe_experts(hs, tile_expert, n_valid, wg, wu, wd, *, tr, tf=1408):
    n_rows, d = hs.shape
    f = wg.shape[2]
    tf = _tile(f, tf, LANES)
    grid_spec = pltpu.PrefetchScalarGridSpec(
        num_scalar_prefetch=2,
        grid=(n_rows // tr, f // tf),
        in_specs=[pl.BlockSpec((tr, d), lambda i, j, te, nv: (i, 0)),
                  pl.BlockSpec((None, d, tf), lambda i, j, te, nv: (te[i], 0, j)),
                  pl.BlockSpec((None, d, tf), lambda i, j, te, nv: (te[i], 0, j)),
                  pl.BlockSpec((None, tf, d), lambda i, j, te, nv: (te[i], j, 0))],
        out_specs=pl.BlockSpec((tr, d), lambda i, j, te, nv: (i, 0)),
        scratch_shapes=[pltpu.VMEM((tr, d), BF16), pltpu.VMEM((tr, d), F32)],
    )
    return pl.pallas_call(
        _experts_kernel,
        grid_spec=grid_spec,
        out_shape=jax.ShapeDtypeStruct((n_rows, d), F32),
        compiler_params=_params(("arbitrary", "arbitrary")),
        name="moe_experts",
    )(tile_expert, n_valid, hs, wg, wu, wd)


def _combine_kernel(dest_ref, ys_ref, meta_ref, x_ref, gt_ref, g_ref, o_ref, buf_ref, sem, *, n_tokens, groups):
    tb = x_ref.shape[0]
    base = pl.program_id(0) * tb

    def row_copy(r, k):
        slot = dest_ref[k * n_tokens + base + r]
        return pltpu.make_async_copy(ys_ref.at[pl.ds(slot, 1), :], buf_ref.at[k, pl.ds(r, 1), :], sem)

    def start(r, carry):
        for k in range(TOP_K):
            row_copy(r, k).start()
        return carry

    def wait(r, carry):
        for k in range(TOP_K):
            row_copy(r, k).wait()
        return carry

    lax.fori_loop(0, tb, start, 0, unroll=DMA_LOOP_UNROLL)
    lax.fori_loop(0, tb, wait, 0, unroll=DMA_LOOP_UNROLL)

    meta = meta_ref[...]
    w1 = _lane_col(meta, META_W1)
    w2 = _lane_col(meta, META_W2)
    g = g_ref[...]
    for gi in range(groups):
        lo, hi = gi * GROUP, (gi + 1) * GROUP
        y = w1[lo:hi] * buf_ref[0, lo:hi, :] + w2[lo:hi] * buf_ref[1, lo:hi, :]
        o_ref[lo:hi, :] = x_ref[lo:hi, :] + gt_ref[gi:gi + 1, :] * _rms(y, g)


def _moe_combine(ys, dest, meta, x, modg, g_post, *, tb=512):
    n, d = x.shape
    tb = _tile(n, tb, GROUP * SUBLANES)
    groups = tb // GROUP
    grid_spec = pltpu.PrefetchScalarGridSpec(
        num_scalar_prefetch=1,
        grid=(n // tb,),
        in_specs=[pl.BlockSpec(memory_space=pl.ANY),
                  pl.BlockSpec((tb, LANES), lambda i, dest: (i, 0)),
                  pl.BlockSpec((tb, d), lambda i, dest: (i, 0)),
                  pl.BlockSpec((groups, d), lambda i, dest: (i, 5)),
                  pl.BlockSpec((1, d), lambda i, dest: (0, 0))],
        out_specs=pl.BlockSpec((tb, d), lambda i, dest: (i, 0)),
        scratch_shapes=[pltpu.VMEM((TOP_K, tb, d), F32), pltpu.SemaphoreType.DMA(())],
    )
    return pl.pallas_call(
        functools.partial(_combine_kernel, n_tokens=n, groups=groups),
        grid_spec=grid_spec,
        out_shape=jax.ShapeDtypeStruct((n, d), F32),
        compiler_params=_params(("arbitrary",)),
        name="moe_combine",
    )(dest, ys, meta, x, modg, g_post.reshape(1, d))


def _moe(x, modg, g_pre, g_post, router, wg, wu, wd, *, tr=512):
    n, d = x.shape
    n_experts = router[3]
    h, meta, counts = _moe_route(x, modg, g_pre, router)

    cnt = counts[0, :n_experts].astype(jnp.int32)
    padded = (cnt + tr - 1) // tr * tr
    ends = jnp.cumsum(padded)
    starts = ends - padded
    n_tiles = -(-TOP_K * n // tr) + n_experts
    experts = jnp.arange(n_experts, dtype=jnp.int32)

    def slot(i_lane, r_lane):
        idx = meta[:, i_lane].astype(jnp.int32)
        seg = jnp.sum(jnp.where(idx[:, None] == experts[None, :], starts[None, :], 0), axis=1)
        return seg + meta[:, r_lane].astype(jnp.int32)

    dest = jnp.concatenate([slot(META_I1, META_R1), slot(META_I2, META_R2)])
    tile_start = jnp.arange(n_tiles, dtype=jnp.int32) * tr
    tile_expert = jnp.minimum(jnp.sum((tile_start[:, None] >= ends[None, :]).astype(jnp.int32), axis=1), n_experts - 1)
    n_valid = (ends[-1] // tr).reshape(1)

    hs = _moe_dispatch(h, dest, n_tiles * tr)
    ys = _moe_experts(hs, tile_expert, n_valid, wg, wu, wd, tr=tr)
    return _moe_combine(ys, dest, meta, x, modg, g_post)


def _unit_lower_inverse_pairs(lows):
    c = lows[0].shape[0]
    h = c // 2
    n_blk = h // SUBLANES
    lane = lax.broadcasted_iota(jnp.int32, (h, 2 * c), 1)
    row = lax.broadcasted_iota(jnp.int32, (h, 2 * c), 0)
    seg = lane // h
    is_00 = (seg == 0) | (seg == 2)
    eye = jnp.where(row == lane - seg * h, 1.0, 0.0)
    seg_start = seg[:SUBLANES] * h
    bottoms = [low[h:] for low in lows]
    diags = [jnp.where(is_00, low[:h], low[h:]) for low in lows]
    low_blk = [[d[r * SUBLANES:(r + 1) * SUBLANES, :] for r in range(n_blk)] for d in diags]
    inv_blk = [[eye[r * SUBLANES:(r + 1) * SUBLANES, :] for r in range(n_blk)] for _ in lows]
    for j in range(h - 1):
        rj = j // SUBLANES
        col_j = seg_start + j
        for p in range(len(lows)):
            pivot_row = inv_blk[p][rj][j % SUBLANES:j % SUBLANES + 1, :]
            for r in range(rj, n_blk):
                inv_blk[p][r] = inv_blk[p][r] - jnp.take_along_axis(low_blk[p][r], col_j, axis=1) * pivot_row
    inv_diag = [jnp.concatenate(blk, axis=0) for blk in inv_blk]
    zero = jnp.zeros_like(eye)

    def rows_of_00(m):
        return jnp.concatenate([jnp.where(seg == 0, m, 0.0), zero, jnp.where(seg == 2, m, 0.0), zero], axis=0)

    def rows_of_11(m):
        return jnp.concatenate([zero, jnp.where(seg == 0, m, 0.0), zero, jnp.where(seg == 2, m, 0.0)], axis=0)

    l10_inv00 = [_dot_split(jnp.where(is_00, b, 0.0), rows_of_00(d)) for b, d in zip(bottoms, inv_diag)]
    inv10 = [-_dot_split(jnp.where(is_00, 0.0, d), rows_of_11(z)) for d, z in zip(inv_diag, l10_inv00)]
    return [jnp.concatenate([jnp.where(is_00, d, 0.0), jnp.where(is_00, y, d)], axis=0)
            for d, y in zip(inv_diag, inv10)]


def _gdn_kernel(seq_ref, first_ref, last_ref,
                proj_ref, ba_ref, ctx_ref, cw_ref, alog_ref, dtb_ref, gain_ref, s0_ref,
                og_ref, sout_ref, tail_ref, xs_ref, st_ref, *, heads, dk, dv):
    del seq_ref
    i = pl.program_id(0)
    c = GROUP
    pad = SUBLANES
    qk_w = heads * dk
    conv_dim = 2 * qk_w + heads * dv

    @pl.when(first_ref[i] == 1)
    def _():
        xs_ref[0:pad, :] = ctx_ref[...]
        st_ref[...] = s0_ref[...]

    @pl.when(first_ref[i] == 0)
    def _():
        xs_ref[0:pad, :] = xs_ref[c:c + pad, :]

    xs_ref[pad:pad + c, :] = proj_ref[:, 0:conv_dim]

    row = lax.broadcasted_iota(jnp.int32, (c, 2 * c), 0)
    lane = lax.broadcasted_iota(jnp.int32, (c, 2 * c), 1)
    is_a = lane < c
    col = jnp.where(is_a, lane, lane - c)
    strict = row > col
    causal = row >= col
    eye = jnp.where(row == col, 1.0, 0.0)
    ba = ba_ref[...]
    gain = gain_ref[...]

    def cols(c0):
        return slice(c0, c0 + LANES)

    def conv(c0):
        acc = None
        for t in range(CONV_W):
            start = pad - (CONV_W - 1) + t
            term = xs_ref[start:start + c, cols(c0)] * cw_ref[t:t + 1, cols(c0)]
            acc = term if acc is None else acc + term
        return _silu(acc)

    def block_diag(m):
        return jnp.concatenate([jnp.where(is_a, m, 0.0), jnp.where(is_a, 0.0, m)], axis=0)

    all_heads = range(heads)
    pairs = range(heads // 2)
    q, k, kb, vb, g = [], [], [], [], []
    for hh in all_heads:
        q_h = conv(hh * dk)
        k_h = conv(qk_w + hh * dk)
        v_h = conv(2 * qk_w + hh * dv)
        q.append(q_h * lax.rsqrt(jnp.sum(q_h * q_h, axis=-1, keepdims=True) + L2_EPS) * (dk ** -0.5))
        k.append(k_h * lax.rsqrt(jnp.sum(k_h * k_h, axis=-1, keepdims=True) + L2_EPS))
        beta = _sigmoid(_lane_col(ba, hh))
        g.append(-jnp.exp(_lane_col(alog_ref[...], hh)) * _softplus(_lane_col(ba, heads + hh)
                                                                     + _lane_col(dtb_ref[...], hh)))
        kb.append(k[-1] * beta)
        vb.append(v_h * beta)
    gc_col = [jnp.sum(jnp.where(causal[:, :c], jnp.sum(jnp.where(eye[:, :c] > 0, g_h, 0.0), axis=0, keepdims=True),
                                0.0), axis=1, keepdims=True) for g_h in g]
    eg = [jnp.exp(gc_h) for gc_h in gc_col]
    decay = []
    for p in pairs:
        a, b = 2 * p, 2 * p + 1
        gc_row = jnp.sum(jnp.where(row <= col, jnp.where(is_a, g[a], g[b]), 0.0), axis=0, keepdims=True)
        decay.append(jnp.exp(jnp.where(causal, jnp.where(is_a, gc_col[a], gc_col[b]) - gc_row, 0.0)))
    gram = [_bdot_nt(jnp.concatenate([kb[2 * p], kb[2 * p + 1], q[2 * p], q[2 * p + 1]], axis=0),
                     jnp.concatenate([k[2 * p], k[2 * p + 1]], axis=0)) for p in pairs]
    lower = [jnp.where(strict, jnp.where(is_a, gram[p][:c], gram[p][c:2 * c]) * decay[p], 0.0) for p in pairs]
    a_qk = [jnp.where(causal, jnp.where(is_a, gram[p][2 * c:3 * c], gram[p][3 * c:]) * decay[p], 0.0) for p in pairs]
    inv = _unit_lower_inverse_pairs(lower)
    rhs = [jnp.concatenate([jnp.concatenate([vb[hh], kb[hh] * eg[hh]], axis=1) for hh in (2 * p, 2 * p + 1)], axis=0)
           for p in pairs]
    uw = [rhs[p] + _bdot(block_diag(inv[p] - eye), rhs[p]) for p in pairs]
    ws_qs = []
    for hh in all_heads:
        w = uw[hh // 2][(hh % 2) * c:(hh % 2 + 1) * c, dv:]
        ws_qs.append(_bdot(jnp.concatenate([w, q[hh] * eg[hh]], axis=0), st_ref[hh]))
    v_new = [uw[hh // 2][(hh % 2) * c:(hh % 2 + 1) * c, :dv] - ws_qs[hh][:c] for hh in all_heads]
    av = [_bdot(block_diag(a_qk[p]), jnp.concatenate([v_new[2 * p], v_new[2 * p + 1]], axis=0)) for p in pairs]
    for hh in all_heads:
        gc_last = gc_col[hh][c - 1:c, :]
        k_tail = k[hh] * jnp.exp(gc_last - gc_col[hh])
        st_ref[hh] = st_ref[hh] * jnp.exp(gc_last) + _bdot_tn(k_tail, v_new[hh])
    for hh in all_heads:
        o = ws_qs[hh][c:] + av[hh // 2][(hh % 2) * c:(hh % 2 + 1) * c]
        z = proj_ref[:, cols(conv_dim + hh * dv)]
        og_ref[:, cols(hh * dv)] = (_rms(o, gain) * _silu(z)).astype(og_ref.dtype)

    @pl.when(last_ref[i] == 1)
    def _():
        sout_ref[...] = st_ref[...]
        tail_ref[...] = xs_ref[c:c + pad, :]


def _gdn_core(proj, ba, ctx8, conv_w, a_log, dt_bias, o_gain, s0, seq_of_chunk, first, last, *, heads, dk, dv):
    n, pw = proj.shape
    conv_dim = 2 * heads * dk + heads * dv
    n_seq = s0.shape[0]
    c = GROUP
    assert n % c == 0 and dk == LANES and dv == LANES and 2 * c == LANES and heads % 2 == 0

    def lane_row(vec):
        return jnp.zeros((1, LANES), F32).at[0, :vec.shape[0]].set(vec.astype(F32))

    grid_spec = pltpu.PrefetchScalarGridSpec(
        num_scalar_prefetch=3,
        grid=(n // c,),
        in_specs=[pl.BlockSpec((c, pw), lambda i, s, f, l: (i, 0)),
                  pl.BlockSpec((c, LANES), lambda i, s, f, l: (i, 0)),
                  pl.BlockSpec((None, SUBLANES, conv_dim), lambda i, s, f, l: (s[i], 0, 0)),
                  pl.BlockSpec((CONV_W, conv_dim), lambda i, s, f, l: (0, 0)),
                  pl.BlockSpec((1, LANES), lambda i, s, f, l: (0, 0)),
                  pl.BlockSpec((1, LANES), lambda i, s, f, l: (0, 0)),
                  pl.BlockSpec((1, dv), lambda i, s, f, l: (0, 0)),
                  pl.BlockSpec((None, heads, dk, dv), lambda i, s, f, l: (s[i], 0, 0, 0))],
        out_specs=[pl.BlockSpec((c, heads * dv), lambda i, s, f, l: (i, 0)),
                   pl.BlockSpec((None, heads, dk, dv), lambda i, s, f, l: (s[i], 0, 0, 0)),
                   pl.BlockSpec((None, SUBLANES, conv_dim), lambda i, s, f, l: (s[i], 0, 0))],
        scratch_shapes=[pltpu.VMEM((c + SUBLANES, conv_dim), F32),
                        pltpu.VMEM((heads, dk, dv), F32)],
    )
    return pl.pallas_call(
        functools.partial(_gdn_kernel, heads=heads, dk=dk, dv=dv),
        grid_spec=grid_spec,
        out_shape=[jax.ShapeDtypeStruct((n, heads * dv), BF16),
                   jax.ShapeDtypeStruct((n_seq, heads, dk, dv), F32),
                   jax.ShapeDtypeStruct((n_seq, SUBLANES, conv_dim), F32)],
        compiler_params=_params(("arbitrary",)),
        name="gdn_core",
    )(seq_of_chunk, first, last, proj, ba, ctx8, conv_w, lane_row(a_log), lane_row(dt_bias),
      o_gain.reshape(1, dv).astype(F32), s0)


def _cumsum_kernel(x_ref, o_ref):
    heads, t = x_ref.shape
    upper = (lax.broadcasted_iota(jnp.int32, (LANES, LANES), 0)
             <= lax.broadcasted_iota(jnp.int32, (LANES, LANES), 1)).astype(BF16)

    def body(i, carry):
        cols = pl.ds(pl.multiple_of(i * LANES, LANES), LANES)
        x = x_ref[:, cols]
        x1 = x.astype(BF16)
        r1 = x - x1.astype(F32)
        x2 = r1.astype(BF16)
        x3 = (r1 - x2.astype(F32)).astype(BF16)
        dot = functools.partial(jnp.dot, preferred_element_type=F32)
        out = carry + (dot(x1, upper) + (dot(x2, upper) + dot(x3, upper)))
        o_ref[:, cols] = out
        return out[:, LANES - 1:LANES]

    lax.fori_loop(0, t // LANES, body, jnp.zeros((heads, 1), F32))


def _cumsum_rows(x):
    b, heads, t = x.shape
    assert t % LANES == 0
    return pl.pallas_call(
        _cumsum_kernel,
        grid=(b,),
        in_specs=[pl.BlockSpec((None, heads, t), lambda i: (i, 0, 0))],
        out_specs=pl.BlockSpec((None, heads, t), lambda i: (i, 0, 0)),
        out_shape=jax.ShapeDtypeStruct((b, heads, t), F32),
        compiler_params=_params(("arbitrary",)),
        name="logf_cumsum",
    )(x)


def _fox_kernel(q_ref, kn_ref, vn_ref, *rest, tq, tc, n_cache, hd):
    if n_cache:
        kc_ref, vc_ref, f_ref, o_ref, kb_ref, vb_ref, *s_bufs = rest
        assert kn_ref.shape[0] == tq
    else:
        f_ref, o_ref, kb_ref, vb_ref, *s_bufs = rest
    qi = pl.program_id(2)
    q_off = n_cache * tc

    @pl.when(qi == 0)
    def _():
        kb_ref[...] = kn_ref[...].astype(BF16)
        vb_ref[...] = vn_ref[...].astype(BF16)

    lane = lax.broadcasted_iota(jnp.int32, (tq, LANES), 1)
    row = lax.broadcasted_iota(jnp.int32, (tq, tq), 0)
    col = lax.broadcasted_iota(jnp.int32, (tq, tq), 1)
    q2 = q_ref[...] * (hd ** -0.5)
    single_q_tile = kn_ref.shape[0] == tq
    q_start = 0 if single_q_tile else pl.multiple_of(qi * tq, tq)
    halves = range(LANES // hd)
    in_half = [(lane >= h * hd) & (lane < (h + 1) * hd) for h in halves]
    q_stack = jnp.concatenate([jnp.where(in_half[h], q2, jnp.zeros_like(q2)) for h in halves], axis=0)
    f_q_rows = [f_ref[h, :, pl.ds(q_off + q_start, tq)] for h in halves]
    f_q = [jnp.sum(jnp.where(row == col, f_q_rows[h], 0.0), axis=1, keepdims=True) for h in halves]

    def scores(k_t):
        return lax.dot_general(q_stack, k_t, (((1,), (1,)), ((), ())), preferred_element_type=F32)

    def consume(carry, s_buf, tk, v_t, f_k_rows, mask):
        stats, probs = [], []
        for h in halves:
            m, l, _ = carry[h]
            s = s_buf[h * tq:(h + 1) * tq, 0:tk] + (f_q[h] - f_k_rows[h])
            if mask is not None:
                s = jnp.where(mask, s, NEG_INF)
            m_new = jnp.maximum(m, jnp.max(s, axis=1, keepdims=True))
            alpha = jnp.exp(m - m_new)
            p = jnp.exp(s - m_new)
            stats.append((m_new, alpha * l + jnp.sum(p, axis=1, keepdims=True), alpha))
            probs.append(p.astype(BF16))
        pv = jnp.dot(jnp.concatenate(probs, axis=0), v_t, preferred_element_type=F32)
        return tuple((stats[h][0], stats[h][1], stats[h][2] * carry[h][2] + pv[h * tq:(h + 1) * tq]) for h in halves)

    carry = tuple((jnp.full((tq, 1), NEG_INF, F32), jnp.zeros((tq, 1), F32), jnp.zeros((tq, LANES), F32))
                  for _ in halves)

    if n_cache:
        for t in range(n_cache + 1):
            cur, nxt = s_bufs[t % 2], s_bufs[(t + 1) % 2]
            if t == 0:
                cur[:, 0:tc] = scores(kc_ref[0:tc, :].astype(BF16))
            if t + 1 < n_cache:
                nxt[:, 0:tc] = scores(kc_ref[(t + 1) * tc:(t + 2) * tc, :].astype(BF16))
            elif t + 1 == n_cache:
                nxt[:, 0:tq] = scores(kb_ref[...])
            if t < n_cache:
                rows = pl.ds(t * tc, tc)
                carry = consume(carry, cur, tc, vc_ref[rows, :].astype(BF16), [f_ref[h, :, rows] for h in halves], None)
            else:
                carry = consume(carry, cur, tq, vb_ref[...], f_q_rows, row >= col)
    else:
        n_q = kn_ref.shape[0] // tq

        def tile_rows(t):
            return pl.ds(pl.multiple_of(jnp.minimum(t, n_q - 1) * tq, tq), tq)

        def consume_tile(carry, s_buf, t, masked):
            rows = tile_rows(t)
            visible = (t * tq + col) <= (q_start + row) if masked else None
            return consume(carry, s_buf, tq, vb_ref[rows, :], [f_ref[h, :, rows] for h in halves], visible)

        s_bufs[0][...] = scores(kb_ref[tile_rows(0), :])

        def pair_body(i, carry, masked):
            s_bufs[1][...] = scores(kb_ref[tile_rows(2 * i + 1), :])
            carry = consume_tile(carry, s_bufs[0], 2 * i, masked)
            s_bufs[0][...] = scores(kb_ref[tile_rows(2 * i + 2), :])
            return consume_tile(carry, s_bufs[1], 2 * i + 1, masked)

        full_pairs = qi // 2
        carry = lax.fori_loop(0, full_pairs, functools.partial(pair_body, masked=False), carry)
        carry = pair_body(full_pairs, carry, masked=True)

    out = carry[0][2] / carry[0][1]
    for h in list(halves)[1:]:
        out = jnp.where(in_half[h], carry[h][2] / carry[h][1], out)
    o_ref[...] = out.astype(o_ref.dtype)


def _fox_attention(q, k_new, v_new, f_rows, row0, n_batch, t_new, cache=None, *, heads, hd, tq, tc=512):
    width = heads * hd
    per_blk = LANES // hd
    assert t_new % tq == 0 and row0 % tq == 0
    assert tq % LANES == 0 or (tq == t_new and cache is not None)
    n_q = t_new // tq
    n_cache = 0
    in_specs = [pl.BlockSpec((tq, LANES), lambda b, h, i: (row0 // tq + b * n_q + i, h)),
                pl.BlockSpec((t_new, LANES), lambda b, h, i: (b, h)),
                pl.BlockSpec((t_new, LANES), lambda b, h, i: (b, h))]
    args = [q, k_new, v_new]
    if cache is not None:
        t_cache = cache[0].shape[1]
        tc = _tile(t_cache, tc, LANES)
        n_cache = t_cache // tc
        in_specs += [pl.BlockSpec((None, t_cache, LANES), lambda b, h, i: (b, 0, h)),
                     pl.BlockSpec((None, t_cache, LANES), lambda b, h, i: (b, 0, h))]
        args += list(cache)
    t_pad = f_rows.shape[-1]
    in_specs.append(pl.BlockSpec((None, per_blk, 1, t_pad), lambda b, h, i: (b, h, 0, 0)))
    args.append(f_rows)
    return pl.pallas_call(
        functools.partial(_fox_kernel, tq=tq, tc=tc, n_cache=n_cache, hd=hd),
        grid=(n_batch, width // LANES, n_q),
        in_specs=in_specs,
        out_specs=pl.BlockSpec((tq, LANES), lambda b, h, i: (b * n_q + i, h)),
        out_shape=jax.ShapeDtypeStruct((n_batch * t_new, width), BF16),
        scratch_shapes=[pltpu.VMEM((t_new, LANES), BF16), pltpu.VMEM((t_new, LANES), BF16)]
        + [pltpu.VMEM((per_blk * tq, max(tq, tc) if cache is not None else tq), F32)] * 2,
        compiler_params=_params(("arbitrary", "arbitrary", "arbitrary")),
        name="fox_cache" if cache is not None else "fox_prompt",
    )(*args)


def kernel(x_prompt, x_sample, c_prompt, c_sample, cache_k, cache_v, cache_logf, state_delta, state_conv, ada_w, ada_b, pre_mix_g, post_mix_g, pre_ffn_g, post_ffn_g, gdn_w_in, gdn_conv_w, gdn_a_log, gdn_dt_bias, gdn_o_gain, gdn_w_out, kv_norm_g, kv_ada_w, kv_ada_b, kv_w, kv_b_f, fox_w_q, fox_w_o, ffn_w_gate, ffn_w_up, ffn_w_down, moe_router_w, moe_router_b, moe_w_gate, moe_w_up, moe_w_down):
    bp, tp, d = x_prompt.shape
    bs, ts, _ = x_sample.shape
    n_p, n_s = bp * tp, bs * ts
    n = n_p + n_s
    depth = ada_w.shape[0]
    n_gdn = gdn_w_in.shape[0]
    gdn_heads, gdn_dk, gdn_dv = state_delta.shape[2:]
    qk_w, v_w = gdn_heads * gdn_dk, gdn_heads * gdn_dv
    conv_dim = 2 * qk_w + v_w
    fox_heads, fox_hd = cache_k.shape[2:]
    fox_w = fox_heads * fox_hd
    t_cache = cache_k.shape[1]
    n_experts = moe_router_w.shape[-1]
    assert tp % GROUP == 0 and ts % GROUP == 0

    x = jnp.concatenate([x_prompt.reshape(n_p, d), x_sample.reshape(n_s, d)], axis=0)
    c = jnp.concatenate([c_prompt, c_sample], axis=0)

    group_batch = np.concatenate([np.repeat(np.arange(bp), tp // GROUP),
                                  bp + np.repeat(np.arange(bs), ts // GROUP)])

    def per_group(m):
        n_l, _, width = m.shape
        rep = lambda a, b, t: jnp.broadcast_to(a[:, :, None, :], (n_l, b, t // GROUP, width)).reshape(n_l, -1, width)
        return jnp.concatenate([rep(m[:, :bp], bp, tp), rep(m[:, bp:], bs, ts)], axis=1)

    modg = per_group(_mod_matmul(c, ada_w, ada_b))
    kv_modg = per_group(_mod_matmul(c, kv_ada_w[None], kv_ada_b[None]))[0]

    seq_of_chunk = jnp.asarray(group_batch, jnp.int32)
    chunk_in_seq = np.concatenate([np.tile(np.arange(tp // GROUP), bp), np.tile(np.arange(ts // GROUP), bs)])
    seq_chunks = np.concatenate([np.full(n_p // GROUP, tp // GROUP), np.full(n_s // GROUP, ts // GROUP)])
    first = jnp.asarray(chunk_in_seq == 0, jnp.int32)
    last = jnp.asarray(chunk_in_seq == seq_chunks - 1, jnp.int32)

    cache_kv = tuple(a.reshape(bs, t_cache, fox_w).astype(BF16) for a in (cache_k, cache_v))
    delta_out, conv_out = [], []
    kv_p = kv_s = logf = f_p = f_s = None
    for l in range(depth):
        g_mix = pre_mix_g[l]
        if l < n_gdn:
            w_in = gdn_w_in[l]
            w_qkvz = w_in[:, :conv_dim + v_w].astype(BF16)
            w_ba = jnp.zeros((d, LANES), BF16).at[:, :2 * gdn_heads].set(w_in[:, conv_dim + v_w:].astype(BF16))
            proj = _norm_matmul(x, modg[l], 0, 1, g_mix, w_qkvz, out_dtype=F32, name="gdn_in_proj")
            ba = _norm_matmul(x, modg[l], 0, 1, g_mix, w_ba, out_dtype=F32, name="gdn_gate_proj")
            ctx = jnp.concatenate([jnp.zeros((bp, CONV_W - 1, conv_dim), F32), state_conv[:, l]], axis=0)
            ctx8 = jnp.pad(ctx, ((0, 0), (SUBLANES - (CONV_W - 1), 0), (0, 0)))
            s0 = jnp.concatenate([jnp.zeros((bp,) + state_delta.shape[2:], F32), state_delta[:, l]], axis=0)
            mixed, s_out, tail = _gdn_core(proj, ba, ctx8, gdn_conv_w[l], gdn_a_log[l], gdn_dt_bias[l], gdn_o_gain[l],
                                     s0, seq_of_chunk, first, last, heads=gdn_heads, dk=gdn_dk, dv=gdn_dv)
            delta_out.append(s_out)
            conv_out.append(tail[:, SUBLANES - (CONV_W - 1):, :])
            w_o = gdn_w_out[l].astype(BF16)
        else:
            j = l - n_gdn
            q = _norm_matmul(x, modg[l], 0, 1, g_mix, fox_w_q[j].astype(BF16), out_dtype=BF16, name="fox_q_proj")
            o_p = _fox_attention(q, kv_p[0], kv_p[1], f_p, 0, bp, tp, heads=fox_heads, hd=fox_hd, tq=min(tp, 256))
            o_s = _fox_attention(q, kv_s[0], kv_s[1], f_s, n_p, bs, ts, cache=cache_kv, heads=fox_heads, hd=fox_hd, tq=ts)
            mixed = jnp.concatenate([o_p, o_s], axis=0)
            w_o = fox_w_o[j].astype(BF16)
        x = _matmul_post(mixed, w_o, x, modg[l], 2, post_mix_g[l], name="mixer_out_proj")

        i = l // 2
        if l % 2 == 0:
            x = _ffn(x, modg[l], pre_ffn_g[l], post_ffn_g[l], ffn_w_gate[i].astype(BF16),
                     ffn_w_up[i].astype(BF16), ffn_w_down[i].astype(BF16), name="dense_ffn")
        else:
            rw = jnp.zeros((d, LANES), F32).at[:, :n_experts].set(moe_router_w[i])
            rw_hi = rw.astype(BF16)
            rw_lo = (rw - rw_hi.astype(F32)).astype(BF16)
            rb = jnp.zeros((1, LANES), F32).at[0, :n_experts].set(moe_router_b[i])
            x = _moe(x, modg[l], pre_ffn_g[l], post_ffn_g[l], (rw_hi, rw_lo, rb, n_experts),
                     moe_w_gate[i].astype(BF16), moe_w_up[i].astype(BF16), moe_w_down[i].astype(BF16))

        if l == n_gdn - 1:
            w_k = kv_w[:, :fox_w].astype(BF16)
            w_v = kv_w[:, fox_w:2 * fox_w].astype(BF16)
            kv_p, kv_s = [[_norm_matmul(x, kv_modg, 0, 1, kv_norm_g, w, out_dtype=F32, rows=rows, name=name)
                           for w, name in ((w_k, "kv_k_proj"), (w_v, "kv_v_proj"))]
                          for rows in ((0, n_p), (n_p, n_s))]
            w_f = jnp.zeros((d, LANES), BF16).at[:, :fox_heads].set(kv_w[:, 2 * fox_w:].astype(BF16))
            b_f = jnp.zeros((LANES,), F32).at[:fox_heads].set(kv_b_f)
            logf = _norm_matmul(x, kv_modg, 0, 1, kv_norm_g, w_f, out_dtype=F32, bias=b_f, name="kv_f_proj")[:, :fox_heads]
            logf_p = logf[:n_p].reshape(bp, tp, fox_heads)
            logf_s = logf[n_p:].reshape(bs, ts, fox_heads)
            f_p = _cumsum_rows(jnp.swapaxes(logf_p, 1, 2))[:, :, None, :]
            rows_s = jnp.swapaxes(jnp.concatenate([cache_logf, logf_s], axis=1), 1, 2)
            t_all = t_cache + ts
            t_pad = -(-t_all // LANES) * LANES
            f_s = _cumsum_rows(jnp.pad(rows_s, ((0, 0), (0, 0), (0, t_pad - t_all))))[:, :, None, :]

    delta = jnp.stack(delta_out, axis=1)
    conv = jnp.stack(conv_out, axis=1)
    return (x[:n_p].reshape(bp, tp, d), x[n_p:].reshape(bs, ts, d),
            kv_p[0].reshape(bp, tp, fox_heads, fox_hd), kv_p[1].reshape(bp, tp, fox_heads, fox_hd),
            logf[:n_p].reshape(bp, tp, fox_heads), delta[:bp], conv[:bp],
            kv_s[0].reshape(bs, ts, fox_heads, fox_hd), kv_s[1].reshape(bs, ts, fox_heads, fox_hd),
            logf[n_p:].reshape(bs, ts, fox_heads), delta[bp:], conv[bp:])
```

```python
import functools
import math

import jax
import jax.numpy as jnp
import numpy as np
from jax import lax
from jax.experimental import pallas as pl
from jax.experimental.pallas import tpu as pltpu

F32 = jnp.float32
BF16 = jnp.bfloat16
NORM_EPS = 1e-6
L2_EPS = 1e-6
GROUP = 64
LANES = 128
SUBLANES = 8
VMEM_LIMIT_BYTES = 56 * 1024 * 1024
CONV_W = 4
TOP_K = 2
NEG_INF = float("-inf")
DMA_LOOP_UNROLL = 8
N_DMA_PRIORITIES = 2
N_COMBINE_PARTS = 2


def _sigmoid(x):
    return 1.0 / (1.0 + jnp.exp(-x))


def _silu(x):
    return x * _sigmoid(x)


def _softplus(x):
    return jnp.maximum(x, 0.0) + jnp.log1p(jnp.exp(-jnp.abs(x)))


def _rms(x, g):
    return x * lax.rsqrt(jnp.mean(x * x, axis=-1, keepdims=True) + NORM_EPS) * g


def _bdot(a, b):
    return jnp.dot(a.astype(BF16), b.astype(BF16), preferred_element_type=F32)


def _bdot_nt(a, b):
    return lax.dot_general(a.astype(BF16), b.astype(BF16), (((1,), (1,)), ((), ())),
                           preferred_element_type=F32)


def _bdot_tn(a, b):
    return lax.dot_general(a.astype(BF16), b.astype(BF16), (((0,), (0,)), ((), ())),
                           preferred_element_type=F32)


def _split2(x):
    hi = x.astype(BF16)
    lo = (x - hi.astype(F32)).astype(BF16)
    return hi, lo


def _dot_split(a, b):
    a_hi, a_lo = _split2(a)
    b_hi, b_lo = _split2(b)
    dot = functools.partial(jnp.dot, preferred_element_type=F32)
    return dot(a_hi, b_hi) + (dot(a_hi, b_lo) + dot(a_lo, b_hi))


def _lane_col(x, idx):
    lane = lax.broadcasted_iota(jnp.int32, x.shape, 1)
    return jnp.sum(jnp.where(lane == idx, x, 0.0), axis=1, keepdims=True)


def _modulated_norm(x_ref, sh_ref, sc_ref, g_ref, groups, emit):
    g = g_ref[...]
    for gi in range(groups):
        rows = pl.ds(gi * GROUP, GROUP)
        h = _rms(x_ref[rows, :], g) * (1.0 + sc_ref[gi:gi + 1, :]) + sh_ref[gi:gi + 1, :]
        emit(rows, h)


def _gated_residual(y_ref, x_ref, gt_ref, g_ref, o_ref, groups):
    g = g_ref[...]
    for gi in range(groups):
        rows = pl.ds(gi * GROUP, GROUP)
        o_ref[rows, :] = x_ref[rows, :] + gt_ref[gi:gi + 1, :] * _rms(y_ref[rows, :], g)


def _tile(n, cap, unit):
    best = None
    for t in range(unit, min(n, cap) + 1, unit):
        if n % t == 0:
            best = t
    assert best is not None, (n, cap, unit)
    return best


def _params(semantics):
    return pltpu.CompilerParams(dimension_semantics=semantics, vmem_limit_bytes=VMEM_LIMIT_BYTES)


def _mod_kernel(c_ref, w_ref, b_ref, o_ref):
    s = _silu(c_ref[...])
    o_ref[...] = _bdot(s, w_ref[...]) + b_ref[...]


def _mod_matmul(c, w, b, tn=1024):
    n_l, d, width = w.shape
    m = c.shape[0]
    tn = _tile(width, tn, LANES)
    return pl.pallas_call(
        _mod_kernel,
        grid=(n_l, width // tn),
        in_specs=[pl.BlockSpec((m, d), lambda l, j: (0, 0)),
                  pl.BlockSpec((None, d, tn), lambda l, j: (l, 0, j)),
                  pl.BlockSpec((None, 1, tn), lambda l, j: (l, 0, j))],
        out_specs=pl.BlockSpec((None, m, tn), lambda l, j: (l, 0, j)),
        out_shape=jax.ShapeDtypeStruct((n_l, m, width), F32),
        compiler_params=_params(("arbitrary", "arbitrary")),
        name="ada_mod",
    )(c, w, b.reshape(n_l, 1, width))


def _norm_mm_kernel(x_ref, sh_ref, sc_ref, g_ref, w_ref, *rest, groups, log_sigmoid):
    if log_sigmoid:
        b_ref, o_ref, h_ref = rest
    else:
        o_ref, h_ref = rest

    @pl.when(pl.program_id(1) == 0)
    def _():
        def emit(rows, h):
            h_ref[rows, :] = h.astype(BF16)
        _modulated_norm(x_ref, sh_ref, sc_ref, g_ref, groups, emit)

    y = jnp.dot(h_ref[...], w_ref[...], preferred_element_type=F32)
    if log_sigmoid:
        y = -_softplus(-(y + b_ref[...]))
    o_ref[...] = y.astype(o_ref.dtype)


def _norm_matmul(x, modg, sh_col, sc_col, g, w, *, out_dtype, bias=None, rows=None, tm=1024, tn=2048, name):
    d = x.shape[1]
    row0, n = rows if rows is not None else (0, x.shape[0])
    width = w.shape[1]
    tm = _tile(math.gcd(n, row0) if row0 else n, tm, GROUP * SUBLANES)
    tn = _tile(width, tn, LANES)
    groups = tm // GROUP
    t0 = row0 // tm
    in_specs = [pl.BlockSpec((tm, d), lambda i, j: (t0 + i, 0)),
                pl.BlockSpec((groups, d), lambda i, j: (t0 + i, sh_col)),
                pl.BlockSpec((groups, d), lambda i, j: (t0 + i, sc_col)),
                pl.BlockSpec((1, d), lambda i, j: (0, 0)),
                pl.BlockSpec((d, tn), lambda i, j: (0, j))]
    args = [x, modg, modg, g.reshape(1, d), w]
    if bias is not None:
        in_specs.append(pl.BlockSpec((1, tn), lambda i, j: (0, j)))
        args.append(bias.reshape(1, width))
    return pl.pallas_call(
        functools.partial(_norm_mm_kernel, groups=groups, log_sigmoid=bias is not None),
        grid=(n // tm, width // tn),
        in_specs=in_specs,
        out_specs=pl.BlockSpec((tm, tn), lambda i, j: (i, j)),
        out_shape=jax.ShapeDtypeStruct((n, width), out_dtype),
        scratch_shapes=[pltpu.VMEM((tm, d), BF16)],
        compiler_params=_params(("arbitrary", "arbitrary")),
        name=name,
    )(*args)


def _mm_post_kernel(a_ref, w_ref, x_ref, gt_ref, g_ref, o_ref, y_ref, *, groups):
    y_ref[...] = jnp.dot(a_ref[...], w_ref[...], preferred_element_type=F32)
    _gated_residual(y_ref, x_ref, gt_ref, g_ref, o_ref, groups)


def _matmul_post(a, w, x, modg, gt_col, g, *, tm=512, name):
    n, d = x.shape
    k = a.shape[1]
    tm = _tile(n, tm, GROUP * SUBLANES)
    groups = tm // GROUP
    return pl.pallas_call(
        functools.partial(_mm_post_kernel, groups=groups),
        grid=(n // tm,),
        in_specs=[pl.BlockSpec((tm, k), lambda i: (i, 0)),
                  pl.BlockSpec((k, d), lambda i: (0, 0)),
                  pl.BlockSpec((tm, d), lambda i: (i, 0)),
                  pl.BlockSpec((groups, d), lambda i: (i, gt_col)),
                  pl.BlockSpec((1, d), lambda i: (0, 0))],
        out_specs=pl.BlockSpec((tm, d), lambda i: (i, 0)),
        out_shape=jax.ShapeDtypeStruct((n, d), F32),
        scratch_shapes=[pltpu.VMEM((tm, d), F32)],
        compiler_params=_params(("arbitrary",)),
        name=name,
    )(a, w, x, modg, g.reshape(1, d))


def _swiglu_step(h, wg_ref, wu_ref, wd_ref):
    gate = jnp.dot(h, wg_ref[...], preferred_element_type=F32)
    up = jnp.dot(h, wu_ref[...], preferred_element_type=F32)
    act = (_silu(gate) * up).astype(BF16)
    return jnp.dot(act, wd_ref[...], preferred_element_type=F32)


def _ffn_kernel(x_ref, sh_ref, sc_ref, gt_ref, gpre_ref, gpost_ref, wg_ref, wu_ref, wd_ref, o_ref, h_ref, acc_ref,
                *, groups):
    j = pl.program_id(1)

    @pl.when(j == 0)
    def _():
        def emit(rows, h):
            h_ref[rows, :] = h.astype(BF16)
        _modulated_norm(x_ref, sh_ref, sc_ref, gpre_ref, groups, emit)
        acc_ref[...] = jnp.zeros_like(acc_ref)

    acc_ref[...] += _swiglu_step(h_ref[...], wg_ref, wu_ref, wd_ref)

    @pl.when(j == pl.num_programs(1) - 1)
    def _():
        _gated_residual(acc_ref, x_ref, gt_ref, gpost_ref, o_ref, groups)


def _ffn(x, modg, g_pre, g_post, wg, wu, wd, *, tm=512, tf=1408, name):
    n, d = x.shape
    f = wg.shape[1]
    tm = _tile(n, tm, GROUP * SUBLANES)
    tf = _tile(f, tf, LANES)
    groups = tm // GROUP
    row = lambda i, j: (i, 0)
    return pl.pallas_call(
        functools.partial(_ffn_kernel, groups=groups),
        grid=(n // tm, f // tf),
        in_specs=[pl.BlockSpec((tm, d), row),
                  pl.BlockSpec((groups, d), lambda i, j: (i, 3)),
                  pl.BlockSpec((groups, d), lambda i, j: (i, 4)),
                  pl.BlockSpec((groups, d), lambda i, j: (i, 5)),
                  pl.BlockSpec((1, d), lambda i, j: (0, 0)),
                  pl.BlockSpec((1, d), lambda i, j: (0, 0)),
                  pl.BlockSpec((d, tf), lambda i, j: (0, j)),
                  pl.BlockSpec((d, tf), lambda i, j: (0, j)),
                  pl.BlockSpec((tf, d), lambda i, j: (j, 0))],
        out_specs=pl.BlockSpec((tm, d), row),
        out_shape=jax.ShapeDtypeStruct((n, d), F32),
        scratch_shapes=[pltpu.VMEM((tm, d), BF16), pltpu.VMEM((tm, d), F32)],
        compiler_params=_params(("arbitrary", "arbitrary")),
        name=name,
    )(x, modg, modg, modg, g_pre.reshape(1, d), g_post.reshape(1, d), wg, wu, wd)


META_I1, META_I2, META_R1, META_R2, META_W1, META_W2 = range(6)


def _top2(logits, n_experts):
    lane = lax.broadcasted_iota(jnp.int32, logits.shape, 1).astype(F32)
    big = float(LANES)
    logits = jnp.where(lane < n_experts, logits, NEG_INF)
    m1 = jnp.max(logits, axis=1, keepdims=True)
    i1 = jnp.min(jnp.where(logits == m1, lane, big), axis=1, keepdims=True)
    rest = jnp.where(lane == i1, NEG_INF, logits)
    m2 = jnp.max(rest, axis=1, keepdims=True)
    i2 = jnp.min(jnp.where(rest == m2, lane, big), axis=1, keepdims=True)
    e2 = jnp.exp(m2 - m1)
    den = 1.0 + e2
    return lane, i1, i2, 1.0 / den, e2 / den


def _route_kernel(x_ref, sh_ref, sc_ref, g_ref, rw_hi_ref, rw_lo_ref, rb_ref,
                  h_ref, meta_ref, cnt_ref, sel_ref, carry_ref, *, groups, n_experts):
    tm = x_ref.shape[0]

    @pl.when(pl.program_id(0) == 0)
    def _():
        carry_ref[...] = jnp.zeros_like(carry_ref)

    def emit(rows, h):
        h_ref[rows, :] = h
        h_hi, h_lo = _split2(h)
        dot = functools.partial(jnp.dot, preferred_element_type=F32)
        logits = dot(h_hi, rw_hi_ref[...]) + (dot(h_hi, rw_lo_ref[...]) + dot(h_lo, rw_hi_ref[...]))
        lane, i1, i2, w1, w2 = _top2(logits + rb_ref[...], n_experts)
        sel_ref[rows, :] = jnp.where((lane == i1) | (lane == i2), 1.0, 0.0).astype(BF16)
        meta_ref[rows, :] = (jnp.where(lane == META_I1, i1, 0.0) + jnp.where(lane == META_I2, i2, 0.0)
                             + jnp.where(lane == META_W1, w1, 0.0) + jnp.where(lane == META_W2, w2, 0.0))
    _modulated_norm(x_ref, sh_ref, sc_ref, g_ref, groups, emit)

    sel = sel_ref[...]
    earlier = (lax.broadcasted_iota(jnp.int32, (tm, tm), 0) > lax.broadcasted_iota(jnp.int32, (tm, tm), 1))
    rank = jnp.dot(earlier.astype(BF16), sel, preferred_element_type=F32) + carry_ref[...]
    meta = meta_ref[...]
    lane = lax.broadcasted_iota(jnp.int32, (tm, LANES), 1).astype(F32)
    r1 = jnp.sum(jnp.where(lane == _lane_col(meta, META_I1), rank, 0.0), axis=1, keepdims=True)
    r2 = jnp.sum(jnp.where(lane == _lane_col(meta, META_I2), rank, 0.0), axis=1, keepdims=True)
    meta_ref[...] = meta + jnp.where(lane == META_R1, r1, 0.0) + jnp.where(lane == META_R2, r2, 0.0)
    carry_ref[...] += jnp.sum(sel.astype(F32), axis=0, keepdims=True)
    cnt_ref[...] = carry_ref[...]


def _moe_route(x, modg, g_pre, router, *, tm=512):
    n, d = x.shape
    rw_hi, rw_lo, rb, n_experts = router
    tm = _tile(n, tm, GROUP * SUBLANES)
    groups = tm // GROUP
    const = lambda i: (0, 0)
    return pl.pallas_call(
        functools.partial(_route_kernel, groups=groups, n_experts=n_experts),
        grid=(n // tm,),
        in_specs=[pl.BlockSpec((tm, d), lambda i: (i, 0)),
                  pl.BlockSpec((groups, d), lambda i: (i, 3)),
                  pl.BlockSpec((groups, d), lambda i: (i, 4)),
                  pl.BlockSpec((1, d), const),
                  pl.BlockSpec((d, LANES), const),
                  pl.BlockSpec((d, LANES), const),
                  pl.BlockSpec((1, LANES), const)],
        out_specs=[pl.BlockSpec((tm, d), lambda i: (i, 0)),
                   pl.BlockSpec((tm, LANES), lambda i: (i, 0)),
                   pl.BlockSpec((1, LANES), const)],
        out_shape=[jax.ShapeDtypeStruct((n, d), F32),
                   jax.ShapeDtypeStruct((n, LANES), F32),
                   jax.ShapeDtypeStruct((1, LANES), F32)],
        scratch_shapes=[pltpu.VMEM((tm, LANES), BF16), pltpu.VMEM((1, LANES), F32)],
        compiler_params=_params(("arbitrary",)),
        name="moe_route",
    )(x, modg, modg, g_pre.reshape(1, d), rw_hi, rw_lo, rb)


def _dispatch_kernel(dest_ref, h_ref, init_ref, hs_ref, sem, *, n_tokens):
    del init_ref
    tb = h_ref.shape[0]
    base = pl.program_id(0) * tb

    def row_copy(r, k):
        slot = dest_ref[k * n_tokens + base + r]
        return pltpu.make_async_copy(h_ref.at[pl.ds(r, 1), :], hs_ref.at[pl.ds(slot, 1), :], sem)

    def start(r, carry):
        for k in range(TOP_K):
            row_copy(r, k).start(priority=k % N_DMA_PRIORITIES)
        return carry

    def wait(r, carry):
        for k in range(TOP_K):
            row_copy(r, k).wait()
        return carry

    lax.fori_loop(0, tb, start, 0, unroll=DMA_LOOP_UNROLL)
    lax.fori_loop(0, tb, wait, 0, unroll=DMA_LOOP_UNROLL)


def _moe_dispatch(h, dest, n_rows, *, tb=512):
    n, d = h.shape
    tb = _tile(n, tb, SUBLANES)
    grid_spec = pltpu.PrefetchScalarGridSpec(
        num_scalar_prefetch=1,
        grid=(n // tb,),
        in_specs=[pl.BlockSpec((tb, d), lambda i, dest: (i, 0)),
                  pl.BlockSpec(memory_space=pl.ANY)],
        out_specs=pl.BlockSpec(memory_space=pl.ANY),
        scratch_shapes=[pltpu.SemaphoreType.DMA(())],
    )
    return pl.pallas_call(
        functools.partial(_dispatch_kernel, n_tokens=n),
        grid_spec=grid_spec,
        out_shape=jax.ShapeDtypeStruct((n_rows, d), h.dtype),
        input_output_aliases={2: 0},
        compiler_params=_params(("arbitrary",)),
        name="moe_dispatch",
    )(dest, h, jnp.zeros((n_rows, d), h.dtype))


def _experts_kernel(tile_expert_ref, n_valid_ref, hs_ref, wg_ref, wu_ref, wd_ref, o_ref, hb_ref, acc_ref):
    del tile_expert_ref
    j = pl.program_id(1)

    @pl.when(pl.program_id(0) < n_valid_ref[0])
    def _():
        @pl.when(j == 0)
        def _():
            hb_ref[...] = hs_ref[...].astype(BF16)
            acc_ref[...] = jnp.zeros_like(acc_ref)

        acc_ref[...] += _swiglu_step(hb_ref[...], wg_ref, wu_ref, wd_ref)

        @pl.when(j == pl.num_programs(1) - 1)
        def _():
            o_ref[...] = acc_ref[...]

    @pl.when(pl.program_id(0) >= n_valid_ref[0])
    def _():
        o_ref[...] = jnp.zeros_like(o_ref)


def _moe_experts(hs, tile_expert, n_valid, wg, wu, wd, *, tr, tf=1408):
    n_rows, d = hs.shape
    f = wg.shape[2]
    tf = _tile(f, tf, LANES)
    grid_spec = pltpu.PrefetchScalarGridSpec(
        num_scalar_prefetch=2,
        grid=(n_rows // tr, f // tf),
        in_specs=[pl.BlockSpec((tr, d), lambda i, j, te, nv: (i, 0)),
                  pl.BlockSpec((None, d, tf), lambda i, j, te, nv: (te[i], 0, j)),
                  pl.BlockSpec((None, d, tf), lambda i, j, te, nv: (te[i], 0, j)),
                  pl.BlockSpec((None, tf, d), lambda i, j, te, nv: (te[i], j, 0))],
        out_specs=pl.BlockSpec((tr, d), lambda i, j, te, nv: (i, 0)),
        scratch_shapes=[pltpu.VMEM((tr, d), BF16), pltpu.VMEM((tr, d), F32)],
    )
    return pl.pallas_call(
        _experts_kernel,
        grid_spec=grid_spec,
        out_shape=jax.ShapeDtypeStruct((n_rows, d), F32),
        compiler_params=_params(("arbitrary", "arbitrary")),
        name="moe_experts",
    )(tile_expert, n_valid, hs, wg, wu, wd)


def _combine_kernel(dest_ref, ys_ref, meta_ref, x_ref, gt_ref, g_ref, o_ref, buf_ref, sem, *, n_tokens, groups):
    tb = x_ref.shape[0]
    base = pl.program_id(0) * tb

    part_rows = tb // N_COMBINE_PARTS
    part_groups = groups // N_COMBINE_PARTS

    def row_copy(r, k, part):
        slot = dest_ref[k * n_tokens + base + r]
        return pltpu.make_async_copy(ys_ref.at[pl.ds(slot, 1), :], buf_ref.at[k, pl.ds(r, 1), :], sem.at[part])

    def start(part, r, carry):
        for k in range(TOP_K):
            row_copy(r, k, part).start(priority=k % N_DMA_PRIORITIES)
        return carry

    def wait(part, r, carry):
        for k in range(TOP_K):
            row_copy(r, k, part).wait()
        return carry

    for part in range(N_COMBINE_PARTS):
        lax.fori_loop(part * part_rows, (part + 1) * part_rows, functools.partial(start, part), 0, unroll=DMA_LOOP_UNROLL)

    meta = meta_ref[...]
    w1 = _lane_col(meta, META_W1)
    w2 = _lane_col(meta, META_W2)
    g = g_ref[...]
    for part in range(N_COMBINE_PARTS):
        lax.fori_loop(part * part_rows, (part + 1) * part_rows, functools.partial(wait, part), 0, unroll=DMA_LOOP_UNROLL)
        for gi in range(part * part_groups, (part + 1) * part_groups):
            lo, hi = gi * GROUP, (gi + 1) * GROUP
            y = w1[lo:hi] * buf_ref[0, lo:hi, :] + w2[lo:hi] * buf_ref[1, lo:hi, :]
            o_ref[lo:hi, :] = x_ref[lo:hi, :] + gt_ref[gi:gi + 1, :] * _rms(y, g)


def _moe_combine(ys, dest, meta, x, modg, g_post, *, tb=512):
    n, d = x.shape
    tb = _tile(n, tb, GROUP * SUBLANES)
    groups = tb // GROUP
    grid_spec = pltpu.PrefetchScalarGridSpec(
        num_scalar_prefetch=1,
        grid=(n // tb,),
        in_specs=[pl.BlockSpec(memory_space=pl.ANY),
                  pl.BlockSpec((tb, LANES), lambda i, dest: (i, 0)),
                  pl.BlockSpec((tb, d), lambda i, dest: (i, 0)),
                  pl.BlockSpec((groups, d), lambda i, dest: (i, 5)),
                  pl.BlockSpec((1, d), lambda i, dest: (0, 0))],
        out_specs=pl.BlockSpec((tb, d), lambda i, dest: (i, 0)),
        scratch_shapes=[pltpu.VMEM((TOP_K, tb, d), F32), pltpu.SemaphoreType.DMA((N_COMBINE_PARTS,))],
    )
    return pl.pallas_call(
        functools.partial(_combine_kernel, n_tokens=n, groups=groups),
        grid_spec=grid_spec,
        out_shape=jax.ShapeDtypeStruct((n, d), F32),
        compiler_params=_params(("arbitrary",)),
        name="moe_combine",
    )(dest, ys, meta, x, modg, g_post.reshape(1, d))


def _moe(x, modg, g_pre, g_post, router, wg, wu, wd, *, tr=512):
    n, d = x.shape
    n_experts = router[3]
    h, meta, counts = _moe_route(x, modg, g_pre, router)

    cnt = counts[0, :n_experts].astype(jnp.int32)
    padded = (cnt + tr - 1) // tr * tr
    ends = jnp.cumsum(padded)
    starts = ends - padded
    n_tiles = -(-TOP_K * n // tr) + n_experts
    experts = jnp.arange(n_experts, dtype=jnp.int32)

    def slot(i_lane, r_lane):
        idx = meta[:, i_lane].astype(jnp.int32)
        seg = jnp.sum(jnp.where(idx[:, None] == experts[None, :], starts[None, :], 0), axis=1)
        return seg + meta[:, r_lane].astype(jnp.int32)

    dest = jnp.concatenate([slot(META_I1, META_R1), slot(META_I2, META_R2)])
    tile_start = jnp.arange(n_tiles, dtype=jnp.int32) * tr
    tile_expert = jnp.minimum(jnp.sum((tile_start[:, None] >= ends[None, :]).astype(jnp.int32), axis=1), n_experts - 1)
    n_valid = (ends[-1] // tr).reshape(1)

    hs = _moe_dispatch(h, dest, n_tiles * tr)
    ys = _moe_experts(hs, tile_expert, n_valid, wg, wu, wd, tr=tr)
    return _moe_combine(ys, dest, meta, x, modg, g_post)


def _unit_lower_inverse_pairs(lows):
    c = lows[0].shape[0]
    h = c // 2
    n_blk = h // SUBLANES
    lane = lax.broadcasted_iota(jnp.int32, (h, 2 * c), 1)
    row = lax.broadcasted_iota(jnp.int32, (h, 2 * c), 0)
    seg = lane // h
    is_00 = (seg == 0) | (seg == 2)
    eye = jnp.where(row == lane - seg * h, 1.0, 0.0)
    seg_start = seg[:SUBLANES] * h
    bottoms = [low[h:] for low in lows]
    diags = [jnp.where(is_00, low[:h], low[h:]) for low in lows]
    low_blk = [[d[r * SUBLANES:(r + 1) * SUBLANES, :] for r in range(n_blk)] for d in diags]
    inv_blk = [[eye[r * SUBLANES:(r + 1) * SUBLANES, :] for r in range(n_blk)] for _ in lows]
    for j in range(h - 1):
        rj = j // SUBLANES
        col_j = seg_start + j
        for p in range(len(lows)):
            pivot_row = inv_blk[p][rj][j % SUBLANES:j % SUBLANES + 1, :]
            for r in range(rj, n_blk):
                inv_blk[p][r] = inv_blk[p][r] - jnp.take_along_axis(low_blk[p][r], col_j, axis=1) * pivot_row
    inv_diag = [jnp.concatenate(blk, axis=0) for blk in inv_blk]
    zero = jnp.zeros_like(eye)

    def rows_of_00(m):
        return jnp.concatenate([jnp.where(seg == 0, m, 0.0), zero, jnp.where(seg == 2, m, 0.0), zero], axis=0)

    def rows_of_11(m):
        return jnp.concatenate([zero, jnp.where(seg == 0, m, 0.0), zero, jnp.where(seg == 2, m, 0.0)], axis=0)

    l10_inv00 = [_dot_split(jnp.where(is_00, b, 0.0), rows_of_00(d)) for b, d in zip(bottoms, inv_diag)]
    inv10 = [-_dot_split(jnp.where(is_00, 0.0, d), rows_of_11(z)) for d, z in zip(inv_diag, l10_inv00)]
    return [jnp.concatenate([jnp.where(is_00, d, 0.0), jnp.where(is_00, y, d)], axis=0)
            for d, y in zip(inv_diag, inv10)]


def _gdn_kernel(seq_ref, first_ref, last_ref,
                proj_ref, ba_ref, ctx_ref, cw_ref, alog_ref, dtb_ref, gain_ref, s0_ref,
                og_ref, sout_ref, tail_ref, xs_ref, st_ref, *, heads, dk, dv):
    del seq_ref
    i = pl.program_id(0)
    c = GROUP
    pad = SUBLANES
    qk_w = heads * dk
    conv_dim = 2 * qk_w + heads * dv

    @pl.when(first_ref[i] == 1)
    def _():
        xs_ref[0:pad, :] = ctx_ref[...]
        st_ref[...] = s0_ref[...]

    @pl.when(first_ref[i] == 0)
    def _():
        xs_ref[0:pad, :] = xs_ref[c:c + pad, :]

    xs_ref[pad:pad + c, :] = proj_ref[:, 0:conv_dim]

    row = lax.broadcasted_iota(jnp.int32, (c, 2 * c), 0)
    lane = lax.broadcasted_iota(jnp.int32, (c, 2 * c), 1)
    is_a = lane < c
    col = jnp.where(is_a, lane, lane - c)
    strict = row > col
    causal = row >= col
    eye = jnp.where(row == col, 1.0, 0.0)
    ba = ba_ref[...]
    gain = gain_ref[...]

    def cols(c0):
        return slice(c0, c0 + LANES)

    def conv(c0):
        acc = None
        for t in range(CONV_W):
            start = pad - (CONV_W - 1) + t
            term = xs_ref[start:start + c, cols(c0)] * cw_ref[t:t + 1, cols(c0)]
            acc = term if acc is None else acc + term
        return _silu(acc)

    def block_diag(m):
        return jnp.concatenate([jnp.where(is_a, m, 0.0), jnp.where(is_a, 0.0, m)], axis=0)

    all_heads = range(heads)
    pairs = range(heads // 2)
    q, k, kb, vb, g = [], [], [], [], []
    for hh in all_heads:
        q_h = conv(hh * dk)
        k_h = conv(qk_w + hh * dk)
        v_h = conv(2 * qk_w + hh * dv)
        q.append(q_h * lax.rsqrt(jnp.sum(q_h * q_h, axis=-1, keepdims=True) + L2_EPS) * (dk ** -0.5))
        k.append(k_h * lax.rsqrt(jnp.sum(k_h * k_h, axis=-1, keepdims=True) + L2_EPS))
        beta = _sigmoid(_lane_col(ba, hh))
        g.append(-jnp.exp(_lane_col(alog_ref[...], hh)) * _softplus(_lane_col(ba, heads + hh)
                                                                     + _lane_col(dtb_ref[...], hh)))
        kb.append(k[-1] * beta)
        vb.append(v_h * beta)
    gc_col = [jnp.sum(jnp.where(causal[:, :c], jnp.sum(jnp.where(eye[:, :c] > 0, g_h, 0.0), axis=0, keepdims=True),
                                0.0), axis=1, keepdims=True) for g_h in g]
    eg = [jnp.exp(gc_h) for gc_h in gc_col]
    decay = []
    for p in pairs:
        a, b = 2 * p, 2 * p + 1
        gc_row = jnp.sum(jnp.where(row <= col, jnp.where(is_a, g[a], g[b]), 0.0), axis=0, keepdims=True)
        decay.append(jnp.exp(jnp.where(causal, jnp.where(is_a, gc_col[a], gc_col[b]) - gc_row, 0.0)))
    gram = [_bdot_nt(jnp.concatenate([kb[2 * p], kb[2 * p + 1], q[2 * p], q[2 * p + 1]], axis=0),
                     jnp.concatenate([k[2 * p], k[2 * p + 1]], axis=0)) for p in pairs]
    lower = [jnp.where(strict, jnp.where(is_a, gram[p][:c], gram[p][c:2 * c]) * decay[p], 0.0) for p in pairs]
    a_qk = [jnp.where(causal, jnp.where(is_a, gram[p][2 * c:3 * c], gram[p][3 * c:]) * decay[p], 0.0) for p in pairs]
    inv = _unit_lower_inverse_pairs(lower)
    rhs = [jnp.concatenate([jnp.concatenate([vb[hh], kb[hh] * eg[hh]], axis=1) for hh in (2 * p, 2 * p + 1)], axis=0)
           for p in pairs]
    uw = [rhs[p] + _bdot(block_diag(inv[p] - eye), rhs[p]) for p in pairs]
    ws_qs = []
    for hh in all_heads:
        w = uw[hh // 2][(hh % 2) * c:(hh % 2 + 1) * c, dv:]
        ws_qs.append(_bdot(jnp.concatenate([w, q[hh] * eg[hh]], axis=0), st_ref[hh]))
    v_new = [uw[hh // 2][(hh % 2) * c:(hh % 2 + 1) * c, :dv] - ws_qs[hh][:c] for hh in all_heads]
    av = [_bdot(block_diag(a_qk[p]), jnp.concatenate([v_new[2 * p], v_new[2 * p + 1]], axis=0)) for p in pairs]
    for hh in all_heads:
        gc_last = gc_col[hh][c - 1:c, :]
        k_tail = k[hh] * jnp.exp(gc_last - gc_col[hh])
        st_ref[hh] = st_ref[hh] * jnp.exp(gc_last) + _bdot_tn(k_tail, v_new[hh])
    for hh in all_heads:
        o = ws_qs[hh][c:] + av[hh // 2][(hh % 2) * c:(hh % 2 + 1) * c]
        z = proj_ref[:, cols(conv_dim + hh * dv)]
        og_ref[:, cols(hh * dv)] = (_rms(o, gain) * _silu(z)).astype(og_ref.dtype)

    @pl.when(last_ref[i] == 1)
    def _():
        sout_ref[...] = st_ref[...]
        tail_ref[...] = xs_ref[c:c + pad, :]


def _gdn_core(proj, ctx8, conv_w, a_log, dt_bias, o_gain, s0, seq_of_chunk, first, last, *, heads, dk, dv):
    n = proj.shape[0]
    pw = proj.shape[1] - LANES
    conv_dim = 2 * heads * dk + heads * dv
    assert pw == conv_dim + heads * dv
    n_seq = s0.shape[0]
    c = GROUP
    assert n % c == 0 and dk == LANES and dv == LANES and 2 * c == LANES and heads % 2 == 0

    def lane_row(vec):
        return jnp.zeros((1, LANES), F32).at[0, :vec.shape[0]].set(vec.astype(F32))

    grid_spec = pltpu.PrefetchScalarGridSpec(
        num_scalar_prefetch=3,
        grid=(n // c,),
        in_specs=[pl.BlockSpec((c, pw), lambda i, s, f, l: (i, 0)),
                  pl.BlockSpec((c, LANES), lambda i, s, f, l: (i, pw // LANES)),
                  pl.BlockSpec((None, SUBLANES, conv_dim), lambda i, s, f, l: (s[i], 0, 0)),
                  pl.BlockSpec((CONV_W, conv_dim), lambda i, s, f, l: (0, 0)),
                  pl.BlockSpec((1, LANES), lambda i, s, f, l: (0, 0)),
                  pl.BlockSpec((1, LANES), lambda i, s, f, l: (0, 0)),
                  pl.BlockSpec((1, dv), lambda i, s, f, l: (0, 0)),
                  pl.BlockSpec((None, heads, dk, dv), lambda i, s, f, l: (s[i], 0, 0, 0))],
        out_specs=[pl.BlockSpec((c, heads * dv), lambda i, s, f, l: (i, 0)),
                   pl.BlockSpec((None, heads, dk, dv), lambda i, s, f, l: (s[i], 0, 0, 0)),
                   pl.BlockSpec((None, SUBLANES, conv_dim), lambda i, s, f, l: (s[i], 0, 0))],
        scratch_shapes=[pltpu.VMEM((c + SUBLANES, conv_dim), F32),
                        pltpu.VMEM((heads, dk, dv), F32)],
    )
    return pl.pallas_call(
        functools.partial(_gdn_kernel, heads=heads, dk=dk, dv=dv),
        grid_spec=grid_spec,
        out_shape=[jax.ShapeDtypeStruct((n, heads * dv), BF16),
                   jax.ShapeDtypeStruct((n_seq, heads, dk, dv), F32),
                   jax.ShapeDtypeStruct((n_seq, SUBLANES, conv_dim), F32)],
        compiler_params=_params(("arbitrary",)),
        name="gdn_core",
    )(seq_of_chunk, first, last, proj, proj, ctx8, conv_w, lane_row(a_log), lane_row(dt_bias),
      o_gain.reshape(1, dv).astype(F32), s0)


def _cumsum_kernel(x_ref, o_ref):
    heads, t = x_ref.shape
    upper = (lax.broadcasted_iota(jnp.int32, (LANES, LANES), 0)
             <= lax.broadcasted_iota(jnp.int32, (LANES, LANES), 1)).astype(BF16)

    def body(i, carry):
        cols = pl.ds(pl.multiple_of(i * LANES, LANES), LANES)
        x = x_ref[:, cols]
        x1 = x.astype(BF16)
        r1 = x - x1.astype(F32)
        x2 = r1.astype(BF16)
        x3 = (r1 - x2.astype(F32)).astype(BF16)
        dot = functools.partial(jnp.dot, preferred_element_type=F32)
        out = carry + (dot(x1, upper) + (dot(x2, upper) + dot(x3, upper)))
        o_ref[:, cols] = out
        return out[:, LANES - 1:LANES]

    lax.fori_loop(0, t // LANES, body, jnp.zeros((heads, 1), F32), unroll=True)


def _cumsum_rows(x):
    b, heads, t = x.shape
    assert t % LANES == 0
    return pl.pallas_call(
        _cumsum_kernel,
        grid=(b,),
        in_specs=[pl.BlockSpec((None, heads, t), lambda i: (i, 0, 0))],
        out_specs=pl.BlockSpec((None, heads, t), lambda i: (i, 0, 0)),
        out_shape=jax.ShapeDtypeStruct((b, heads, t), F32),
        compiler_params=_params(("arbitrary",)),
        name="logf_cumsum",
    )(x)


def _fox_kernel(q_ref, kn_ref, vn_ref, *rest, tq, tc, n_cache, hd):
    if n_cache:
        kc_ref, vc_ref, f_ref, o_ref, kb_ref, vb_ref, *s_bufs = rest
        assert kn_ref.shape[0] == tq
    else:
        f_ref, o_ref, kb_ref, vb_ref, *s_bufs = rest
    qi = pl.program_id(2)
    q_off = n_cache * tc

    @pl.when(qi == 0)
    def _():
        kb_ref[...] = kn_ref[...].astype(BF16)
        vb_ref[...] = vn_ref[...].astype(BF16)

    lane = lax.broadcasted_iota(jnp.int32, (tq, LANES), 1)
    row = lax.broadcasted_iota(jnp.int32, (tq, tq), 0)
    col = lax.broadcasted_iota(jnp.int32, (tq, tq), 1)
    q2 = q_ref[...] * (hd ** -0.5)
    single_q_tile = kn_ref.shape[0] == tq
    q_start = 0 if single_q_tile else pl.multiple_of(qi * tq, tq)
    halves = range(LANES // hd)
    in_half = [(lane >= h * hd) & (lane < (h + 1) * hd) for h in halves]
    q_stack = jnp.concatenate([jnp.where(in_half[h], q2, jnp.zeros_like(q2)) for h in halves], axis=0)
    f_q_rows = [f_ref[h, :, pl.ds(q_off + q_start, tq)] for h in halves]
    f_q = [jnp.sum(jnp.where(row == col, f_q_rows[h], 0.0), axis=1, keepdims=True) for h in halves]

    def scores(k_t):
        return lax.dot_general(q_stack, k_t, (((1,), (1,)), ((), ())), preferred_element_type=F32)

    def consume(carry, s_buf, tk, v_t, f_k_rows, mask):
        stats, probs = [], []
        for h in halves:
            m, l, _ = carry[h]
            s = s_buf[h * tq:(h + 1) * tq, 0:tk] + (f_q[h] - f_k_rows[h])
            if mask is not None:
                s = jnp.where(mask, s, NEG_INF)
            m_new = jnp.maximum(m, jnp.max(s, axis=1, keepdims=True))
            alpha = jnp.exp(m - m_new)
            p = jnp.exp(s - m_new)
            stats.append((m_new, alpha * l + jnp.sum(p, axis=1, keepdims=True), alpha))
            probs.append(p.astype(BF16))
        pv = jnp.dot(jnp.concatenate(probs, axis=0), v_t, preferred_element_type=F32)
        return tuple((stats[h][0], stats[h][1], stats[h][2] * carry[h][2] + pv[h * tq:(h + 1) * tq]) for h in halves)

    carry = tuple((jnp.full((tq, 1), NEG_INF, F32), jnp.zeros((tq, 1), F32), jnp.zeros((tq, LANES), F32))
                  for _ in halves)

    if n_cache:
        for t in range(n_cache + 1):
            cur, nxt = s_bufs[t % 2], s_bufs[(t + 1) % 2]
            if t == 0:
                cur[:, 0:tc] = scores(kc_ref[0:tc, :].astype(BF16))
            if t + 1 < n_cache:
                nxt[:, 0:tc] = scores(kc_ref[(t + 1) * tc:(t + 2) * tc, :].astype(BF16))
            elif t + 1 == n_cache:
                nxt[:, 0:tq] = scores(kb_ref[...])
            if t < n_cache:
                rows = pl.ds(t * tc, tc)
                carry = consume(carry, cur, tc, vc_ref[rows, :].astype(BF16), [f_ref[h, :, rows] for h in halves], None)
            else:
                carry = consume(carry, cur, tq, vb_ref[...], f_q_rows, row >= col)
    else:
        n_q = kn_ref.shape[0] // tq

        def tile_rows(t):
            return pl.ds(pl.multiple_of(jnp.minimum(t, n_q - 1) * tq, tq), tq)

        def consume_tile(carry, s_buf, t, masked):
            rows = tile_rows(t)
            visible = (t * tq + col) <= (q_start + row) if masked else None
            return consume(carry, s_buf, tq, vb_ref[rows, :], [f_ref[h, :, rows] for h in halves], visible)

        s_bufs[0][...] = scores(kb_ref[tile_rows(0), :])

        def pair_body(i, carry, masked):
            s_bufs[1][...] = scores(kb_ref[tile_rows(2 * i + 1), :])
            carry = consume_tile(carry, s_bufs[0], 2 * i, masked)
            s_bufs[0][...] = scores(kb_ref[tile_rows(2 * i + 2), :])
            return consume_tile(carry, s_bufs[1], 2 * i + 1, masked)

        full_pairs = qi // 2
        carry = lax.fori_loop(0, full_pairs, functools.partial(pair_body, masked=False), carry)
        carry = pair_body(full_pairs, carry, masked=True)

    out = carry[0][2] / carry[0][1]
    for h in list(halves)[1:]:
        out = jnp.where(in_half[h], carry[h][2] / carry[h][1], out)
    o_ref[...] = out.astype(o_ref.dtype)


def _fox_attention(q, k_new, v_new, f_rows, row0, n_batch, t_new, cache=None, *, heads, hd, tq, tc=512):
    width = heads * hd
    per_blk = LANES // hd
    assert t_new % tq == 0 and row0 % tq == 0
    assert tq % LANES == 0 or (tq == t_new and cache is not None)
    n_q = t_new // tq
    n_cache = 0
    in_specs = [pl.BlockSpec((tq, LANES), lambda b, h, i: (row0 // tq + b * n_q + i, h)),
                pl.BlockSpec((t_new, LANES), lambda b, h, i: (b, h)),
                pl.BlockSpec((t_new, LANES), lambda b, h, i: (b, h))]
    args = [q, k_new, v_new]
    if cache is not None:
        t_cache = cache[0].shape[1]
        tc = _tile(t_cache, tc, LANES)
        n_cache = t_cache // tc
        in_specs += [pl.BlockSpec((None, t_cache, LANES), lambda b, h, i: (b, 0, h)),
                     pl.BlockSpec((None, t_cache, LANES), lambda b, h, i: (b, 0, h))]
        args += list(cache)
    t_pad = f_rows.shape[-1]
    in_specs.append(pl.BlockSpec((None, per_blk, 1, t_pad), lambda b, h, i: (b, h, 0, 0)))
    args.append(f_rows)
    return pl.pallas_call(
        functools.partial(_fox_kernel, tq=tq, tc=tc, n_cache=n_cache, hd=hd),
        grid=(n_batch, width // LANES, n_q),
        in_specs=in_specs,
        out_specs=pl.BlockSpec((tq, LANES), lambda b, h, i: (b * n_q + i, h)),
        out_shape=jax.ShapeDtypeStruct((n_batch * t_new, width), BF16),
        scratch_shapes=[pltpu.VMEM((t_new, LANES), BF16), pltpu.VMEM((t_new, LANES), BF16)]
        + [pltpu.VMEM((per_blk * tq, max(tq, tc) if cache is not None else tq), F32)] * 2,
        compiler_params=_params(("arbitrary", "arbitrary", "arbitrary")),
        name="fox_cache" if cache is not None else "fox_prompt",
    )(*args)


def kernel(x_prompt, x_sample, c_prompt, c_sample, cache_k, cache_v, cache_logf, state_delta, state_conv, ada_w, ada_b, pre_mix_g, post_mix_g, pre_ffn_g, post_ffn_g, gdn_w_in, gdn_conv_w, gdn_a_log, gdn_dt_bias, gdn_o_gain, gdn_w_out, kv_norm_g, kv_ada_w, kv_ada_b, kv_w, kv_b_f, fox_w_q, fox_w_o, ffn_w_gate, ffn_w_up, ffn_w_down, moe_router_w, moe_router_b, moe_w_gate, moe_w_up, moe_w_down):
    bp, tp, d = x_prompt.shape
    bs, ts, _ = x_sample.shape
    n_p, n_s = bp * tp, bs * ts
    n = n_p + n_s
    depth = ada_w.shape[0]
    n_gdn = gdn_w_in.shape[0]
    gdn_heads, gdn_dk, gdn_dv = state_delta.shape[2:]
    qk_w, v_w = gdn_heads * gdn_dk, gdn_heads * gdn_dv
    conv_dim = 2 * qk_w + v_w
    fox_heads, fox_hd = cache_k.shape[2:]
    fox_w = fox_heads * fox_hd
    t_cache = cache_k.shape[1]
    n_experts = moe_router_w.shape[-1]
    assert tp % GROUP == 0 and ts % GROUP == 0

    x = jnp.concatenate([x_prompt.reshape(n_p, d), x_sample.reshape(n_s, d)], axis=0)
    c = jnp.concatenate([c_prompt, c_sample], axis=0)

    group_batch = np.concatenate([np.repeat(np.arange(bp), tp // GROUP),
                                  bp + np.repeat(np.arange(bs), ts // GROUP)])

    def per_group(m):
        n_l, _, width = m.shape
        rep = lambda a, b, t: jnp.broadcast_to(a[:, :, None, :], (n_l, b, t // GROUP, width)).reshape(n_l, -1, width)
        return jnp.concatenate([rep(m[:, :bp], bp, tp), rep(m[:, bp:], bs, ts)], axis=1)

    modg = per_group(_mod_matmul(c, ada_w, ada_b))
    kv_modg = per_group(_mod_matmul(c, kv_ada_w[None], kv_ada_b[None]))[0]

    seq_of_chunk = jnp.asarray(group_batch, jnp.int32)
    chunk_in_seq = np.concatenate([np.tile(np.arange(tp // GROUP), bp), np.tile(np.arange(ts // GROUP), bs)])
    seq_chunks = np.concatenate([np.full(n_p // GROUP, tp // GROUP), np.full(n_s // GROUP, ts // GROUP)])
    first = jnp.asarray(chunk_in_seq == 0, jnp.int32)
    last = jnp.asarray(chunk_in_seq == seq_chunks - 1, jnp.int32)

    cache_kv = tuple(a.reshape(bs, t_cache, fox_w).astype(BF16) for a in (cache_k, cache_v))
    delta_out, conv_out = [], []
    kv_p = kv_s = logf = f_p = f_s = None
    for l in range(depth):
        g_mix = pre_mix_g[l]
        if l < n_gdn:
            in_w = gdn_w_in.shape[2]
            w_in = jnp.pad(gdn_w_in[l].astype(BF16), ((0, 0), (0, conv_dim + v_w + LANES - in_w)))
            proj = _norm_matmul(x, modg[l], 0, 1, g_mix, w_in, out_dtype=F32, tn=1408, name="gdn_in_proj")
            ctx = jnp.concatenate([jnp.zeros((bp, CONV_W - 1, conv_dim), F32), state_conv[:, l]], axis=0)
            ctx8 = jnp.pad(ctx, ((0, 0), (SUBLANES - (CONV_W - 1), 0), (0, 0)))
            s0 = jnp.concatenate([jnp.zeros((bp,) + state_delta.shape[2:], F32), state_delta[:, l]], axis=0)
            mixed, s_out, tail = _gdn_core(proj, ctx8, gdn_conv_w[l], gdn_a_log[l], gdn_dt_bias[l], gdn_o_gain[l],
                                     s0, seq_of_chunk, first, last, heads=gdn_heads, dk=gdn_dk, dv=gdn_dv)
            delta_out.append(s_out)
            conv_out.append(tail[:, SUBLANES - (CONV_W - 1):, :])
            w_o = gdn_w_out[l].astype(BF16)
        else:
            j = l - n_gdn
            q = _norm_matmul(x, modg[l], 0, 1, g_mix, fox_w_q[j].astype(BF16), out_dtype=BF16, name="fox_q_proj")
            o_p = _fox_attention(q, kv_p[0], kv_p[1], f_p, 0, bp, tp, heads=fox_heads, hd=fox_hd, tq=min(tp, 256))
            o_s = _fox_attention(q, kv_s[0], kv_s[1], f_s, n_p, bs, ts, cache=cache_kv, heads=fox_heads, hd=fox_hd, tq=ts)
            mixed = jnp.concatenate([o_p, o_s], axis=0)
            w_o = fox_w_o[j].astype(BF16)
        x = _matmul_post(mixed, w_o, x, modg[l], 2, post_mix_g[l], name="mixer_out_proj")

        i = l // 2
        if l % 2 == 0:
            x = _ffn(x, modg[l], pre_ffn_g[l], post_ffn_g[l], ffn_w_gate[i].astype(BF16),
                     ffn_w_up[i].astype(BF16), ffn_w_down[i].astype(BF16), name="dense_ffn")
        else:
            rw = jnp.zeros((d, LANES), F32).at[:, :n_experts].set(moe_router_w[i])
            rw_hi = rw.astype(BF16)
            rw_lo = (rw - rw_hi.astype(F32)).astype(BF16)
            rb = jnp.zeros((1, LANES), F32).at[0, :n_experts].set(moe_router_b[i])
            x = _moe(x, modg[l], pre_ffn_g[l], post_ffn_g[l], (rw_hi, rw_lo, rb, n_experts),
                     moe_w_gate[i].astype(BF16), moe_w_up[i].astype(BF16), moe_w_down[i].astype(BF16))

        if l == n_gdn - 1:
            w_k = kv_w[:, :fox_w].astype(BF16)
            w_v = kv_w[:, fox_w:2 * fox_w].astype(BF16)
            kv_p, kv_s = [[_norm_matmul(x, kv_modg, 0, 1, kv_norm_g, w, out_dtype=F32, rows=rows, name=name)
                           for w, name in ((w_k, "kv_k_proj"), (w_v, "kv_v_proj"))]
                          for rows in ((0, n_p), (n_p, n_s))]
            w_f = jnp.zeros((d, LANES), BF16).at[:, :fox_heads].set(kv_w[:, 2 * fox_w:].astype(BF16))
            b_f = jnp.zeros((LANES,), F32).at[:fox_heads].set(kv_b_f)
            logf = _norm_matmul(x, kv_modg, 0, 1, kv_norm_g, w_f, out_dtype=F32, bias=b_f, name="kv_f_proj")[:, :fox_heads]
            logf_p = logf[:n_p].reshape(bp, tp, fox_heads)
            logf_s = logf[n_p:].reshape(bs, ts, fox_heads)
            f_p = _cumsum_rows(jnp.swapaxes(logf_p, 1, 2))[:, :, None, :]
            rows_s = jnp.swapaxes(jnp.concatenate([cache_logf, logf_s], axis=1), 1, 2)
            t_all = t_cache + ts
            t_pad = -(-t_all // LANES) * LANES
            f_s = _cumsum_rows(jnp.pad(rows_s, ((0, 0), (0, 0), (0, t_pad - t_all))))[:, :, None, :]

    delta = jnp.stack(delta_out, axis=1)
    conv = jnp.stack(conv_out, axis=1)
    return (x[:n_p].reshape(bp, tp, d), x[n_p:].reshape(bs, ts, d),
            kv_p[0].reshape(bp, tp, fox_heads, fox_hd), kv_p[1].reshape(bp, tp, fox_heads, fox_hd),
            logf[:n_p].reshape(bp, tp, fox_heads), delta[:bp], conv[:bp],
            kv_s[0].reshape(bs, ts, fox_heads, fox_hd), kv_s[1].reshape(bs, ts, fox_heads, fox_hd),
            logf[n_p:].reshape(bs, ts, fox_heads), delta[bp:], conv[bp:])
```

```python
import functools
import math

import jax
import jax.numpy as jnp
import numpy as np
from jax import lax
from jax.experimental import pallas as pl
from jax.experimental.pallas import tpu as pltpu

F32 = jnp.float32
BF16 = jnp.bfloat16
NORM_EPS = 1e-6
L2_EPS = 1e-6
GROUP = 64
LANES = 128
SUBLANES = 8
VMEM_LIMIT_BYTES = 56 * 1024 * 1024
CONV_W = 4
TOP_K = 2
NEG_INF = float("-inf")
DMA_LOOP_UNROLL = 8
N_DMA_PRIORITIES = 2
N_COMBINE_PARTS = 2


def _sigmoid(x):
    return 1.0 / (1.0 + jnp.exp(-x))


def _silu(x):
    return x * _sigmoid(x)


def _softplus(x):
    return jnp.maximum(x, 0.0) + jnp.log1p(jnp.exp(-jnp.abs(x)))


def _rms(x, g):
    return x * lax.rsqrt(jnp.mean(x * x, axis=-1, keepdims=True) + NORM_EPS) * g


def _bdot(a, b):
    return jnp.dot(a.astype(BF16), b.astype(BF16), preferred_element_type=F32)


def _bdot_nt(a, b):
    return lax.dot_general(a.astype(BF16), b.astype(BF16), (((1,), (1,)), ((), ())),
                           preferred_element_type=F32)


def _bdot_tn(a, b):
    return lax.dot_general(a.astype(BF16), b.astype(BF16), (((0,), (0,)), ((), ())),
                           preferred_element_type=F32)


def _split2(x):
    hi = x.astype(BF16)
    lo = (x - hi.astype(F32)).astype(BF16)
    return hi, lo


def _dot_split(a, b):
    a_hi, a_lo = _split2(a)
    b_hi, b_lo = _split2(b)
    dot = functools.partial(jnp.dot, preferred_element_type=F32)
    return dot(a_hi, b_hi) + (dot(a_hi, b_lo) + dot(a_lo, b_hi))


def _lane_col(x, idx):
    lane = lax.broadcasted_iota(jnp.int32, x.shape, 1)
    return jnp.sum(jnp.where(lane == idx, x, 0.0), axis=1, keepdims=True)


def _modulated_norm(x_ref, sh_ref, sc_ref, g_ref, groups, emit):
    g = g_ref[...]
    for gi in range(groups):
        rows = pl.ds(gi * GROUP, GROUP)
        h = _rms(x_ref[rows, :], g) * (1.0 + sc_ref[gi:gi + 1, :]) + sh_ref[gi:gi + 1, :]
        emit(rows, h)


def _gated_residual(y_ref, x_ref, gt_ref, g_ref, o_ref, groups):
    g = g_ref[...]
    for gi in range(groups):
        rows = pl.ds(gi * GROUP, GROUP)
        o_ref[rows, :] = x_ref[rows, :] + gt_ref[gi:gi + 1, :] * _rms(y_ref[rows, :], g)


def _tile(n, cap, unit):
    best = None
    for t in range(unit, min(n, cap) + 1, unit):
        if n % t == 0:
            best = t
    assert best is not None, (n, cap, unit)
    return best


def _params(semantics):
    return pltpu.CompilerParams(dimension_semantics=semantics, vmem_limit_bytes=VMEM_LIMIT_BYTES)


def _mod_kernel(c_ref, w_ref, b_ref, o_ref):
    s = _silu(c_ref[...])
    o_ref[...] = _bdot(s, w_ref[...]) + b_ref[...]


def _mod_matmul(c, w, b, tn=1024):
    n_l, d, width = w.shape
    m = c.shape[0]
    tn = _tile(width, tn, LANES)
    return pl.pallas_call(
        _mod_kernel,
        grid=(n_l, width // tn),
        in_specs=[pl.BlockSpec((m, d), lambda l, j: (0, 0)),
                  pl.BlockSpec((None, d, tn), lambda l, j: (l, 0, j)),
                  pl.BlockSpec((None, 1, tn), lambda l, j: (l, 0, j))],
        out_specs=pl.BlockSpec((None, m, tn), lambda l, j: (l, 0, j)),
        out_shape=jax.ShapeDtypeStruct((n_l, m, width), F32),
        compiler_params=_params(("arbitrary", "arbitrary")),
        name="ada_mod",
    )(c, w, b.reshape(n_l, 1, width))


def _norm_mm_kernel(x_ref, sh_ref, sc_ref, g_ref, w_ref, o_ref, h_ref, *, groups):
    @pl.when(pl.program_id(1) == 0)
    def _():
        def emit(rows, h):
            h_ref[rows, :] = h.astype(BF16)
        _modulated_norm(x_ref, sh_ref, sc_ref, g_ref, groups, emit)

    o_ref[...] = jnp.dot(h_ref[...], w_ref[...], preferred_element_type=F32).astype(o_ref.dtype)


def _norm_matmul(x, modg, sh_col, sc_col, g, w, *, out_dtype, tm=1024, tn=2048, name):
    n, d = x.shape
    width = w.shape[1]
    tm = _tile(n, tm, GROUP * SUBLANES)
    tn = _tile(width, tn, LANES)
    groups = tm // GROUP
    return pl.pallas_call(
        functools.partial(_norm_mm_kernel, groups=groups),
        grid=(n // tm, width // tn),
        in_specs=[pl.BlockSpec((tm, d), lambda i, j: (i, 0)),
                  pl.BlockSpec((groups, d), lambda i, j: (i, sh_col)),
                  pl.BlockSpec((groups, d), lambda i, j: (i, sc_col)),
                  pl.BlockSpec((1, d), lambda i, j: (0, 0)),
                  pl.BlockSpec((d, tn), lambda i, j: (0, j))],
        out_specs=pl.BlockSpec((tm, tn), lambda i, j: (i, j)),
        out_shape=jax.ShapeDtypeStruct((n, width), out_dtype),
        scratch_shapes=[pltpu.VMEM((tm, d), BF16)],
        compiler_params=_params(("arbitrary", "arbitrary")),
        name=name,
    )(x, modg, modg, g.reshape(1, d), w)


def _kv_proj_kernel(x_ref, sh_ref, sc_ref, g_ref, wk_ref, wv_ref, wf_ref, bf_ref, k_ref, v_ref, f_ref, h_ref, *, groups):
    def emit(rows, h):
        h_ref[rows, :] = h.astype(BF16)
    _modulated_norm(x_ref, sh_ref, sc_ref, g_ref, groups, emit)
    h = h_ref[...]
    k_ref[...] = jnp.dot(h, wk_ref[...], preferred_element_type=F32)
    v_ref[...] = jnp.dot(h, wv_ref[...], preferred_element_type=F32)
    f_ref[...] = -_softplus(-(jnp.dot(h, wf_ref[...], preferred_element_type=F32) + bf_ref[...]))


def _kv_proj(x, modg, g, w_k, w_v, w_f, b_f, rows, *, tm=1024):
    d = x.shape[1]
    row0, n = rows
    width = w_k.shape[1]
    tm = _tile(math.gcd(n, row0) if row0 else n, tm, GROUP * SUBLANES)
    groups = tm // GROUP
    t0 = row0 // tm
    const = lambda i: (0, 0)
    return pl.pallas_call(
        functools.partial(_kv_proj_kernel, groups=groups),
        grid=(n // tm,),
        in_specs=[pl.BlockSpec((tm, d), lambda i: (t0 + i, 0)),
                  pl.BlockSpec((groups, d), lambda i: (t0 + i, 0)),
                  pl.BlockSpec((groups, d), lambda i: (t0 + i, 1)),
                  pl.BlockSpec((1, d), const),
                  pl.BlockSpec((d, width), const),
                  pl.BlockSpec((d, width), const),
                  pl.BlockSpec((d, LANES), const),
                  pl.BlockSpec((1, LANES), const)],
        out_specs=[pl.BlockSpec((tm, width), lambda i: (i, 0)),
                   pl.BlockSpec((tm, width), lambda i: (i, 0)),
                   pl.BlockSpec((tm, LANES), lambda i: (i, 0))],
        out_shape=[jax.ShapeDtypeStruct((n, width), F32),
                   jax.ShapeDtypeStruct((n, width), F32),
                   jax.ShapeDtypeStruct((n, LANES), F32)],
        scratch_shapes=[pltpu.VMEM((tm, d), BF16)],
        compiler_params=_params(("arbitrary",)),
        name="kv_proj",
    )(x, modg, modg, g.reshape(1, d), w_k, w_v, w_f, b_f.reshape(1, LANES))


def _mm_post_kernel(a_ref, w_ref, x_ref, gt_ref, g_ref, o_ref, y_ref, *, groups):
    y_ref[...] = jnp.dot(a_ref[...], w_ref[...], preferred_element_type=F32)
    _gated_residual(y_ref, x_ref, gt_ref, g_ref, o_ref, groups)


def _matmul_post(a, w, x, modg, gt_col, g, *, tm=1024, name):
    n, d = x.shape
    k = a.shape[1]
    tm = _tile(n, tm, GROUP * SUBLANES)
    groups = tm // GROUP
    return pl.pallas_call(
        functools.partial(_mm_post_kernel, groups=groups),
        grid=(n // tm,),
        in_specs=[pl.BlockSpec((tm, k), lambda i: (i, 0)),
                  pl.BlockSpec((k, d), lambda i: (0, 0)),
                  pl.BlockSpec((tm, d), lambda i: (i, 0)),
                  pl.BlockSpec((groups, d), lambda i: (i, gt_col)),
                  pl.BlockSpec((1, d), lambda i: (0, 0))],
        out_specs=pl.BlockSpec((tm, d), lambda i: (i, 0)),
        out_shape=jax.ShapeDtypeStruct((n, d), F32),
        scratch_shapes=[pltpu.VMEM((tm, d), F32)],
        compiler_params=_params(("arbitrary",)),
        name=name,
    )(a, w, x, modg, g.reshape(1, d))


def _swiglu_step(h, wg_ref, wu_ref, wd_ref):
    gate = jnp.dot(h, wg_ref[...], preferred_element_type=F32)
    up = jnp.dot(h, wu_ref[...], preferred_element_type=F32)
    act = (_silu(gate) * up).astype(BF16)
    return jnp.dot(act, wd_ref[...], preferred_element_type=F32)


def _ffn_kernel(x_ref, sh_ref, sc_ref, gt_ref, gpre_ref, gpost_ref, wg_ref, wu_ref, wd_ref, o_ref, h_ref, acc_ref,
                *, groups):
    j = pl.program_id(1)

    @pl.when(j == 0)
    def _():
        def emit(rows, h):
            h_ref[rows, :] = h.astype(BF16)
        _modulated_norm(x_ref, sh_ref, sc_ref, gpre_ref, groups, emit)
        acc_ref[...] = jnp.zeros_like(acc_ref)

    acc_ref[...] += _swiglu_step(h_ref[...], wg_ref, wu_ref, wd_ref)

    @pl.when(j == pl.num_programs(1) - 1)
    def _():
        _gated_residual(acc_ref, x_ref, gt_ref, gpost_ref, o_ref, groups)


def _ffn(x, modg, g_pre, g_post, wg, wu, wd, *, tm=512, tf=1408, name):
    n, d = x.shape
    f = wg.shape[1]
    tm = _tile(n, tm, GROUP * SUBLANES)
    tf = _tile(f, tf, LANES)
    groups = tm // GROUP
    row = lambda i, j: (i, 0)
    return pl.pallas_call(
        functools.partial(_ffn_kernel, groups=groups),
        grid=(n // tm, f // tf),
        in_specs=[pl.BlockSpec((tm, d), row),
                  pl.BlockSpec((groups, d), lambda i, j: (i, 3)),
                  pl.BlockSpec((groups, d), lambda i, j: (i, 4)),
                  pl.BlockSpec((groups, d), lambda i, j: (i, 5)),
                  pl.BlockSpec((1, d), lambda i, j: (0, 0)),
                  pl.BlockSpec((1, d), lambda i, j: (0, 0)),
                  pl.BlockSpec((d, tf), lambda i, j: (0, j)),
                  pl.BlockSpec((d, tf), lambda i, j: (0, j)),
                  pl.BlockSpec((tf, d), lambda i, j: (j, 0))],
        out_specs=pl.BlockSpec((tm, d), row),
        out_shape=jax.ShapeDtypeStruct((n, d), F32),
        scratch_shapes=[pltpu.VMEM((tm, d), BF16), pltpu.VMEM((tm, d), F32)],
        compiler_params=_params(("arbitrary", "arbitrary")),
        name=name,
    )(x, modg, modg, modg, g_pre.reshape(1, d), g_post.reshape(1, d), wg, wu, wd)


META_I1, META_I2, META_R1, META_R2, META_W1, META_W2 = range(6)


def _top2(logits, n_experts):
    lane = lax.broadcasted_iota(jnp.int32, logits.shape, 1).astype(F32)
    big = float(LANES)
    logits = jnp.where(lane < n_experts, logits, NEG_INF)
    m1 = jnp.max(logits, axis=1, keepdims=True)
    i1 = jnp.min(jnp.where(logits == m1, lane, big), axis=1, keepdims=True)
    rest = jnp.where(lane == i1, NEG_INF, logits)
    m2 = jnp.max(rest, axis=1, keepdims=True)
    i2 = jnp.min(jnp.where(rest == m2, lane, big), axis=1, keepdims=True)
    e2 = jnp.exp(m2 - m1)
    den = 1.0 + e2
    return lane, i1, i2, 1.0 / den, e2 / den


def _route_kernel(x_ref, sh_ref, sc_ref, g_ref, rw_hi_ref, rw_lo_ref, rb_ref,
                  h_ref, meta_ref, cnt_ref, sel_ref, carry_ref, *, groups, n_experts):
    tm = x_ref.shape[0]

    @pl.when(pl.program_id(0) == 0)
    def _():
        carry_ref[...] = jnp.zeros_like(carry_ref)

    def emit(rows, h):
        h_ref[rows, :] = h
        h_hi, h_lo = _split2(h)
        dot = functools.partial(jnp.dot, preferred_element_type=F32)
        logits = dot(h_hi, rw_hi_ref[...]) + (dot(h_hi, rw_lo_ref[...]) + dot(h_lo, rw_hi_ref[...]))
        lane, i1, i2, w1, w2 = _top2(logits + rb_ref[...], n_experts)
        sel_ref[rows, :] = jnp.where((lane == i1) | (lane == i2), 1.0, 0.0).astype(BF16)
        meta_ref[rows, :] = (jnp.where(lane == META_I1, i1, 0.0) + jnp.where(lane == META_I2, i2, 0.0)
                             + jnp.where(lane == META_W1, w1, 0.0) + jnp.where(lane == META_W2, w2, 0.0))
    _modulated_norm(x_ref, sh_ref, sc_ref, g_ref, groups, emit)

    sel = sel_ref[...]
    earlier = (lax.broadcasted_iota(jnp.int32, (tm, tm), 0) > lax.broadcasted_iota(jnp.int32, (tm, tm), 1))
    rank = jnp.dot(earlier.astype(BF16), sel, preferred_element_type=F32) + carry_ref[...]
    meta = meta_ref[...]
    lane = lax.broadcasted_iota(jnp.int32, (tm, LANES), 1).astype(F32)
    r1 = jnp.sum(jnp.where(lane == _lane_col(meta, META_I1), rank, 0.0), axis=1, keepdims=True)
    r2 = jnp.sum(jnp.where(lane == _lane_col(meta, META_I2), rank, 0.0), axis=1, keepdims=True)
    meta_ref[...] = meta + jnp.where(lane == META_R1, r1, 0.0) + jnp.where(lane == META_R2, r2, 0.0)
    carry_ref[...] += jnp.sum(sel.astype(F32), axis=0, keepdims=True)
    cnt_ref[...] = carry_ref[...]


def _moe_route(x, modg, g_pre, router, *, tm=512):
    n, d = x.shape
    rw_hi, rw_lo, rb, n_experts = router
    tm = _tile(n, tm, GROUP * SUBLANES)
    groups = tm // GROUP
    const = lambda i: (0, 0)
    return pl.pallas_call(
        functools.partial(_route_kernel, groups=groups, n_experts=n_experts),
        grid=(n // tm,),
        in_specs=[pl.BlockSpec((tm, d), lambda i: (i, 0)),
                  pl.BlockSpec((groups, d), lambda i: (i, 3)),
                  pl.BlockSpec((groups, d), lambda i: (i, 4)),
                  pl.BlockSpec((1, d), const),
                  pl.BlockSpec((d, LANES), const),
                  pl.BlockSpec((d, LANES), const),
                  pl.BlockSpec((1, LANES), const)],
        out_specs=[pl.BlockSpec((tm, d), lambda i: (i, 0)),
                   pl.BlockSpec((tm, LANES), lambda i: (i, 0)),
                   pl.BlockSpec((1, LANES), const)],
        out_shape=[jax.ShapeDtypeStruct((n, d), F32),
                   jax.ShapeDtypeStruct((n, LANES), F32),
                   jax.ShapeDtypeStruct((1, LANES), F32)],
        scratch_shapes=[pltpu.VMEM((tm, LANES), BF16), pltpu.VMEM((1, LANES), F32)],
        compiler_params=_params(("arbitrary",)),
        name="moe_route",
    )(x, modg, modg, g_pre.reshape(1, d), rw_hi, rw_lo, rb)


def _dispatch_kernel(dest_ref, h_ref, init_ref, hs_ref, sem, *, n_tokens):
    del init_ref
    tb = h_ref.shape[0]
    base = pl.program_id(0) * tb

    def row_copy(r, k):
        slot = dest_ref[k * n_tokens + base + r]
        return pltpu.make_async_copy(h_ref.at[pl.ds(r, 1), :], hs_ref.at[pl.ds(slot, 1), :], sem)

    def start(r, carry):
        for k in range(TOP_K):
            row_copy(r, k).start(priority=k % N_DMA_PRIORITIES)
        return carry

    def wait(r, carry):
        for k in range(TOP_K):
            row_copy(r, k).wait()
        return carry

    lax.fori_loop(0, tb, start, 0, unroll=DMA_LOOP_UNROLL)
    lax.fori_loop(0, tb, wait, 0, unroll=DMA_LOOP_UNROLL)


def _moe_dispatch(h, dest, n_rows, *, tb=512):
    n, d = h.shape
    tb = _tile(n, tb, SUBLANES)
    grid_spec = pltpu.PrefetchScalarGridSpec(
        num_scalar_prefetch=1,
        grid=(n // tb,),
        in_specs=[pl.BlockSpec((tb, d), lambda i, dest: (i, 0)),
                  pl.BlockSpec(memory_space=pl.ANY)],
        out_specs=pl.BlockSpec(memory_space=pl.ANY),
        scratch_shapes=[pltpu.SemaphoreType.DMA(())],
    )
    return pl.pallas_call(
        functools.partial(_dispatch_kernel, n_tokens=n),
        grid_spec=grid_spec,
        out_shape=jax.ShapeDtypeStruct((n_rows, d), h.dtype),
        input_output_aliases={2: 0},
        compiler_params=_params(("arbitrary",)),
        name="moe_dispatch",
    )(dest, h, jnp.zeros((n_rows, d), h.dtype))


def _experts_kernel(tile_expert_ref, n_valid_ref, hs_ref, wg_ref, wu_ref, wd_ref, o_ref, hb_ref, acc_ref):
    del tile_expert_ref
    j = pl.program_id(1)

    @pl.when(pl.program_id(0) < n_valid_ref[0])
    def _():
        @pl.when(j == 0)
        def _():
            hb_ref[...] = hs_ref[...].astype(BF16)
            acc_ref[...] = jnp.zeros_like(acc_ref)

        acc_ref[...] += _swiglu_step(hb_ref[...], wg_ref, wu_ref, wd_ref)

        @pl.when(j == pl.num_programs(1) - 1)
        def _():
            o_ref[...] = acc_ref[...]

    @pl.when(pl.program_id(0) >= n_valid_ref[0])
    def _():
        o_ref[...] = jnp.zeros_like(o_ref)


def _moe_experts(hs, tile_expert, n_valid, wg, wu, wd, *, tr, tf=1408):
    n_rows, d = hs.shape
    f = wg.shape[2]
    tf = _tile(f, tf, LANES)
    grid_spec = pltpu.PrefetchScalarGridSpec(
        num_scalar_prefetch=2,
        grid=(n_rows // tr, f // tf),
        in_specs=[pl.BlockSpec((tr, d), lambda i, j, te, nv: (i, 0)),
                  pl.BlockSpec((None, d, tf), lambda i, j, te, nv: (te[i], 0, j)),
                  pl.BlockSpec((None, d, tf), lambda i, j, te, nv: (te[i], 0, j)),
                  pl.BlockSpec((None, tf, d), lambda i, j, te, nv: (te[i], j, 0))],
        out_specs=pl.BlockSpec((tr, d), lambda i, j, te, nv: (i, 0)),
        scratch_shapes=[pltpu.VMEM((tr, d), BF16), pltpu.VMEM((tr, d), F32)],
    )
    return pl.pallas_call(
        _experts_kernel,
        grid_spec=grid_spec,
        out_shape=jax.ShapeDtypeStruct((n_rows, d), F32),
        compiler_params=_params(("arbitrary", "arbitrary")),
        name="moe_experts",
    )(tile_expert, n_valid, hs, wg, wu, wd)


def _combine_kernel(dest_ref, ys_ref, meta_ref, x_ref, gt_ref, g_ref, o_ref, buf_ref, sem, *, n_tokens, groups):
    tb = x_ref.shape[0]
    base = pl.program_id(0) * tb

    part_rows = tb // N_COMBINE_PARTS
    part_groups = groups // N_COMBINE_PARTS

    def row_copy(r, k, part):
        slot = dest_ref[k * n_tokens + base + r]
        return pltpu.make_async_copy(ys_ref.at[pl.ds(slot, 1), :], buf_ref.at[k, pl.ds(r, 1), :], sem.at[part])

    def start(part, r, carry):
        for k in range(TOP_K):
            row_copy(r, k, part).start(priority=k % N_DMA_PRIORITIES)
        return carry

    def wait(part, r, carry):
        for k in range(TOP_K):
            row_copy(r, k, part).wait()
        return carry

    for part in range(N_COMBINE_PARTS):
        lax.fori_loop(part * part_rows, (part + 1) * part_rows, functools.partial(start, part), 0, unroll=DMA_LOOP_UNROLL)

    meta = meta_ref[...]
    w1 = _lane_col(meta, META_W1)
    w2 = _lane_col(meta, META_W2)
    g = g_ref[...]
    for part in range(N_COMBINE_PARTS):
        lax.fori_loop(part * part_rows, (part + 1) * part_rows, functools.partial(wait, part), 0, unroll=DMA_LOOP_UNROLL)
        for gi in range(part * part_groups, (part + 1) * part_groups):
            lo, hi = gi * GROUP, (gi + 1) * GROUP
            y = w1[lo:hi] * buf_ref[0, lo:hi, :] + w2[lo:hi] * buf_ref[1, lo:hi, :]
            o_ref[lo:hi, :] = x_ref[lo:hi, :] + gt_ref[gi:gi + 1, :] * _rms(y, g)


def _moe_combine(ys, dest, meta, x, modg, g_post, *, tb=512):
    n, d = x.shape
    tb = _tile(n, tb, GROUP * SUBLANES)
    groups = tb // GROUP
    grid_spec = pltpu.PrefetchScalarGridSpec(
        num_scalar_prefetch=1,
        grid=(n // tb,),
        in_specs=[pl.BlockSpec(memory_space=pl.ANY),
                  pl.BlockSpec((tb, LANES), lambda i, dest: (i, 0)),
                  pl.BlockSpec((tb, d), lambda i, dest: (i, 0)),
                  pl.BlockSpec((groups, d), lambda i, dest: (i, 5)),
                  pl.BlockSpec((1, d), lambda i, dest: (0, 0))],
        out_specs=pl.BlockSpec((tb, d), lambda i, dest: (i, 0)),
        scratch_shapes=[pltpu.VMEM((TOP_K, tb, d), F32), pltpu.SemaphoreType.DMA((N_COMBINE_PARTS,))],
    )
    return pl.pallas_call(
        functools.partial(_combine_kernel, n_tokens=n, groups=groups),
        grid_spec=grid_spec,
        out_shape=jax.ShapeDtypeStruct((n, d), F32),
        compiler_params=_params(("arbitrary",)),
        name="moe_combine",
    )(dest, ys, meta, x, modg, g_post.reshape(1, d))


def _moe(x, modg, g_pre, g_post, router, wg, wu, wd, *, tr=512):
    n, d = x.shape
    n_experts = router[3]
    h, meta, counts = _moe_route(x, modg, g_pre, router)

    cnt = counts[0, :n_experts].astype(jnp.int32)
    padded = (cnt + tr - 1) // tr * tr
    ends = jnp.cumsum(padded)
    starts = ends - padded
    n_tiles = -(-TOP_K * n // tr) + n_experts
    experts = jnp.arange(n_experts, dtype=jnp.int32)

    def slot(i_lane, r_lane):
        idx = meta[:, i_lane].astype(jnp.int32)
        seg = jnp.sum(jnp.where(idx[:, None] == experts[None, :], starts[None, :], 0), axis=1)
        return seg + meta[:, r_lane].astype(jnp.int32)

    dest = jnp.concatenate([slot(META_I1, META_R1), slot(META_I2, META_R2)])
    tile_start = jnp.arange(n_tiles, dtype=jnp.int32) * tr
    tile_expert = jnp.minimum(jnp.sum((tile_start[:, None] >= ends[None, :]).astype(jnp.int32), axis=1), n_experts - 1)
    n_valid = (ends[-1] // tr).reshape(1)

    hs = _moe_dispatch(h, dest, n_tiles * tr)
    ys = _moe_experts(hs, tile_expert, n_valid, wg, wu, wd, tr=tr)
    return _moe_combine(ys, dest, meta, x, modg, g_post)


def _unit_lower_inverse_pairs(lows):
    c = lows[0].shape[0]
    h = c // 2
    n_blk = h // SUBLANES
    lane = lax.broadcasted_iota(jnp.int32, (h, 2 * c), 1)
    row = lax.broadcasted_iota(jnp.int32, (h, 2 * c), 0)
    seg = lane // h
    is_00 = (seg == 0) | (seg == 2)
    eye = jnp.where(row == lane - seg * h, 1.0, 0.0)
    seg_start = seg[:SUBLANES] * h
    bottoms = [low[h:] for low in lows]
    diags = [jnp.where(is_00, low[:h], low[h:]) for low in lows]
    low_blk = [[d[r * SUBLANES:(r + 1) * SUBLANES, :] for r in range(n_blk)] for d in diags]
    inv_blk = [[eye[r * SUBLANES:(r + 1) * SUBLANES, :] for r in range(n_blk)] for _ in lows]
    for j in range(h - 1):
        rj = j // SUBLANES
        col_j = seg_start + j
        for p in range(len(lows)):
            pivot_row = inv_blk[p][rj][j % SUBLANES:j % SUBLANES + 1, :]
            for r in range(rj, n_blk):
                inv_blk[p][r] = inv_blk[p][r] - jnp.take_along_axis(low_blk[p][r], col_j, axis=1) * pivot_row
    inv_diag = [jnp.concatenate(blk, axis=0) for blk in inv_blk]
    zero = jnp.zeros_like(eye)

    def rows_of_00(m):
        return jnp.concatenate([jnp.where(seg == 0, m, 0.0), zero, jnp.where(seg == 2, m, 0.0), zero], axis=0)

    def rows_of_11(m):
        return jnp.concatenate([zero, jnp.where(seg == 0, m, 0.0), zero, jnp.where(seg == 2, m, 0.0)], axis=0)

    l10_inv00 = [_dot_split(jnp.where(is_00, b, 0.0), rows_of_00(d)) for b, d in zip(bottoms, inv_diag)]
    inv10 = [-_dot_split(jnp.where(is_00, 0.0, d), rows_of_11(z)) for d, z in zip(inv_diag, l10_inv00)]
    return [jnp.concatenate([jnp.where(is_00, d, 0.0), jnp.where(is_00, y, d)], axis=0)
            for d, y in zip(inv_diag, inv10)]


def _gdn_kernel(seq_ref, first_ref, last_ref,
                proj_ref, ba_ref, ctx_ref, cw_ref, alog_ref, dtb_ref, gain_ref, s0_ref,
                og_ref, sout_ref, tail_ref, xs_ref, st_ref, *, heads, dk, dv):
    del seq_ref
    i = pl.program_id(0)
    c = GROUP
    pad = SUBLANES
    qk_w = heads * dk
    conv_dim = 2 * qk_w + heads * dv

    @pl.when(first_ref[i] == 1)
    def _():
        xs_ref[0:pad, :] = ctx_ref[...]
        st_ref[...] = s0_ref[...]

    @pl.when(first_ref[i] == 0)
    def _():
        xs_ref[0:pad, :] = xs_ref[c:c + pad, :]

    xs_ref[pad:pad + c, :] = proj_ref[:, 0:conv_dim]

    row = lax.broadcasted_iota(jnp.int32, (c, 2 * c), 0)
    lane = lax.broadcasted_iota(jnp.int32, (c, 2 * c), 1)
    is_a = lane < c
    col = jnp.where(is_a, lane, lane - c)
    strict = row > col
    causal = row >= col
    eye = jnp.where(row == col, 1.0, 0.0)
    ba = ba_ref[...]
    gain = gain_ref[...]

    def cols(c0):
        return slice(c0, c0 + LANES)

    def conv(c0):
        acc = None
        for t in range(CONV_W):
            start = pad - (CONV_W - 1) + t
            term = xs_ref[start:start + c, cols(c0)] * cw_ref[t:t + 1, cols(c0)]
            acc = term if acc is None else acc + term
        return _silu(acc)

    def block_diag(m):
        return jnp.concatenate([jnp.where(is_a, m, 0.0), jnp.where(is_a, 0.0, m)], axis=0)

    all_heads = range(heads)
    pairs = range(heads // 2)
    q, k, kb, vb, g = [], [], [], [], []
    for hh in all_heads:
        q_h = conv(hh * dk)
        k_h = conv(qk_w + hh * dk)
        v_h = conv(2 * qk_w + hh * dv)
        q.append(q_h * lax.rsqrt(jnp.sum(q_h * q_h, axis=-1, keepdims=True) + L2_EPS) * (dk ** -0.5))
        k.append(k_h * lax.rsqrt(jnp.sum(k_h * k_h, axis=-1, keepdims=True) + L2_EPS))
        beta = _sigmoid(_lane_col(ba, hh))
        g.append(-jnp.exp(_lane_col(alog_ref[...], hh)) * _softplus(_lane_col(ba, heads + hh)
                                                                     + _lane_col(dtb_ref[...], hh)))
        kb.append(k[-1] * beta)
        vb.append(v_h * beta)
    gc_col = [jnp.sum(jnp.where(causal[:, :c], jnp.sum(jnp.where(eye[:, :c] > 0, g_h, 0.0), axis=0, keepdims=True),
                                0.0), axis=1, keepdims=True) for g_h in g]
    eg = [jnp.exp(gc_h) for gc_h in gc_col]
    decay = []
    for p in pairs:
        a, b = 2 * p, 2 * p + 1
        gc_row = jnp.sum(jnp.where(row <= col, jnp.where(is_a, g[a], g[b]), 0.0), axis=0, keepdims=True)
        decay.append(jnp.exp(jnp.where(causal, jnp.where(is_a, gc_col[a], gc_col[b]) - gc_row, 0.0)))
    gram = [_bdot_nt(jnp.concatenate([kb[2 * p], kb[2 * p + 1], q[2 * p], q[2 * p + 1]], axis=0),
                     jnp.concatenate([k[2 * p], k[2 * p + 1]], axis=0)) for p in pairs]
    lower = [jnp.where(strict, jnp.where(is_a, gram[p][:c], gram[p][c:2 * c]) * decay[p], 0.0) for p in pairs]
    a_qk = [jnp.where(causal, jnp.where(is_a, gram[p][2 * c:3 * c], gram[p][3 * c:]) * decay[p], 0.0) for p in pairs]
    inv = _unit_lower_inverse_pairs(lower)
    rhs = [jnp.concatenate([jnp.concatenate([vb[hh], kb[hh] * eg[hh]], axis=1) for hh in (2 * p, 2 * p + 1)], axis=0)
           for p in pairs]
    uw = [rhs[p] + _bdot(block_diag(inv[p] - eye), rhs[p]) for p in pairs]
    ws_qs = []
    for hh in all_heads:
        w = uw[hh // 2][(hh % 2) * c:(hh % 2 + 1) * c, dv:]
        ws_qs.append(_bdot(jnp.concatenate([w, q[hh] * eg[hh]], axis=0), st_ref[hh]))
    v_new = [uw[hh // 2][(hh % 2) * c:(hh % 2 + 1) * c, :dv] - ws_qs[hh][:c] for hh in all_heads]
    av = [_bdot(block_diag(a_qk[p]), jnp.concatenate([v_new[2 * p], v_new[2 * p + 1]], axis=0)) for p in pairs]
    for hh in all_heads:
        gc_last = gc_col[hh][c - 1:c, :]
        k_tail = k[hh] * jnp.exp(gc_last - gc_col[hh])
        st_ref[hh] = st_ref[hh] * jnp.exp(gc_last) + _bdot_tn(k_tail, v_new[hh])
    for hh in all_heads:
        o = ws_qs[hh][c:] + av[hh // 2][(hh % 2) * c:(hh % 2 + 1) * c]
        z = proj_ref[:, cols(conv_dim + hh * dv)]
        og_ref[:, cols(hh * dv)] = (_rms(o, gain) * _silu(z)).astype(og_ref.dtype)

    @pl.when(last_ref[i] == 1)
    def _():
        sout_ref[...] = st_ref[...]
        tail_ref[...] = xs_ref[c:c + pad, :]


def _gdn_core(proj, ctx8, conv_w, a_log, dt_bias, o_gain, s0, seq_of_chunk, first, last, *, heads, dk, dv):
    n = proj.shape[0]
    pw = proj.shape[1] - LANES
    conv_dim = 2 * heads * dk + heads * dv
    assert pw == conv_dim + heads * dv
    n_seq = s0.shape[0]
    c = GROUP
    assert n % c == 0 and dk == LANES and dv == LANES and 2 * c == LANES and heads % 2 == 0

    def lane_row(vec):
        return jnp.zeros((1, LANES), F32).at[0, :vec.shape[0]].set(vec.astype(F32))

    grid_spec = pltpu.PrefetchScalarGridSpec(
        num_scalar_prefetch=3,
        grid=(n // c,),
        in_specs=[pl.BlockSpec((c, pw), lambda i, s, f, l: (i, 0)),
                  pl.BlockSpec((c, LANES), lambda i, s, f, l: (i, pw // LANES)),
                  pl.BlockSpec((None, SUBLANES, conv_dim), lambda i, s, f, l: (s[i], 0, 0)),
                  pl.BlockSpec((CONV_W, conv_dim), lambda i, s, f, l: (0, 0)),
                  pl.BlockSpec((1, LANES), lambda i, s, f, l: (0, 0)),
                  pl.BlockSpec((1, LANES), lambda i, s, f, l: (0, 0)),
                  pl.BlockSpec((1, dv), lambda i, s, f, l: (0, 0)),
                  pl.BlockSpec((None, heads, dk, dv), lambda i, s, f, l: (s[i], 0, 0, 0))],
        out_specs=[pl.BlockSpec((c, heads * dv), lambda i, s, f, l: (i, 0)),
                   pl.BlockSpec((None, heads, dk, dv), lambda i, s, f, l: (s[i], 0, 0, 0)),
                   pl.BlockSpec((None, SUBLANES, conv_dim), lambda i, s, f, l: (s[i], 0, 0))],
        scratch_shapes=[pltpu.VMEM((c + SUBLANES, conv_dim), F32),
                        pltpu.VMEM((heads, dk, dv), F32)],
    )
    return pl.pallas_call(
        functools.partial(_gdn_kernel, heads=heads, dk=dk, dv=dv),
        grid_spec=grid_spec,
        out_shape=[jax.ShapeDtypeStruct((n, heads * dv), BF16),
                   jax.ShapeDtypeStruct((n_seq, heads, dk, dv), F32),
                   jax.ShapeDtypeStruct((n_seq, SUBLANES, conv_dim), F32)],
        compiler_params=_params(("arbitrary",)),
        name="gdn_core",
    )(seq_of_chunk, first, last, proj, proj, ctx8, conv_w, lane_row(a_log), lane_row(dt_bias),
      o_gain.reshape(1, dv).astype(F32), s0)


def _cumsum_kernel(x_ref, o_ref):
    heads, t = x_ref.shape
    upper = (lax.broadcasted_iota(jnp.int32, (LANES, LANES), 0)
             <= lax.broadcasted_iota(jnp.int32, (LANES, LANES), 1)).astype(BF16)

    def body(i, carry):
        cols = pl.ds(pl.multiple_of(i * LANES, LANES), LANES)
        x = x_ref[:, cols]
        x1 = x.astype(BF16)
        r1 = x - x1.astype(F32)
        x2 = r1.astype(BF16)
        x3 = (r1 - x2.astype(F32)).astype(BF16)
        dot = functools.partial(jnp.dot, preferred_element_type=F32)
        out = carry + (dot(x1, upper) + (dot(x2, upper) + dot(x3, upper)))
        o_ref[:, cols] = out
        return out[:, LANES - 1:LANES]

    lax.fori_loop(0, t // LANES, body, jnp.zeros((heads, 1), F32), unroll=True)


def _cumsum_rows(x):
    b, heads, t = x.shape
    assert t % LANES == 0
    return pl.pallas_call(
        _cumsum_kernel,
        grid=(b,),
        in_specs=[pl.BlockSpec((None, heads, t), lambda i: (i, 0, 0))],
        out_specs=pl.BlockSpec((None, heads, t), lambda i: (i, 0, 0)),
        out_shape=jax.ShapeDtypeStruct((b, heads, t), F32),
        compiler_params=_params(("arbitrary",)),
        name="logf_cumsum",
    )(x)


def _fox_kernel(q_ref, kn_ref, vn_ref, *rest, tq, tc, n_cache, hd):
    if n_cache:
        kc_ref, vc_ref, f_ref, o_ref, kb_ref, vb_ref, *s_bufs = rest
        assert kn_ref.shape[0] == tq
    else:
        f_ref, o_ref, kb_ref, vb_ref, *s_bufs = rest
    qi = pl.program_id(2)
    q_off = n_cache * tc

    @pl.when(qi == 0)
    def _():
        kb_ref[...] = kn_ref[...].astype(BF16)
        vb_ref[...] = vn_ref[...].astype(BF16)

    lane = lax.broadcasted_iota(jnp.int32, (tq, LANES), 1)
    row = lax.broadcasted_iota(jnp.int32, (tq, tq), 0)
    col = lax.broadcasted_iota(jnp.int32, (tq, tq), 1)
    q2 = q_ref[...] * (hd ** -0.5)
    single_q_tile = kn_ref.shape[0] == tq
    q_start = 0 if single_q_tile else pl.multiple_of(qi * tq, tq)
    halves = range(LANES // hd)
    in_half = [(lane >= h * hd) & (lane < (h + 1) * hd) for h in halves]
    q_stack = jnp.concatenate([jnp.where(in_half[h], q2, jnp.zeros_like(q2)) for h in halves], axis=0)
    f_q_rows = [f_ref[h, :, pl.ds(q_off + q_start, tq)] for h in halves]
    f_q = [jnp.sum(jnp.where(row == col, f_q_rows[h], 0.0), axis=1, keepdims=True) for h in halves]

    def scores(k_t):
        return lax.dot_general(q_stack, k_t, (((1,), (1,)), ((), ())), preferred_element_type=F32)

    def consume(carry, s_buf, tk, v_t, f_k_rows, mask):
        stats, probs = [], []
        for h in halves:
            m, l, _ = carry[h]
            s = s_buf[h * tq:(h + 1) * tq, 0:tk] + (f_q[h] - f_k_rows[h])
            if mask is not None:
                s = jnp.where(mask, s, NEG_INF)
            m_new = jnp.maximum(m, jnp.max(s, axis=1, keepdims=True))
            alpha = jnp.exp(m - m_new)
            p = jnp.exp(s - m_new)
            stats.append((m_new, alpha * l + jnp.sum(p, axis=1, keepdims=True), alpha))
            probs.append(p.astype(BF16))
        pv = jnp.dot(jnp.concatenate(probs, axis=0), v_t, preferred_element_type=F32)
        return tuple((stats[h][0], stats[h][1], stats[h][2] * carry[h][2] + pv[h * tq:(h + 1) * tq]) for h in halves)

    carry = tuple((jnp.full((tq, 1), NEG_INF, F32), jnp.zeros((tq, 1), F32), jnp.zeros((tq, LANES), F32))
                  for _ in halves)

    if n_cache:
        for t in range(n_cache + 1):
            cur, nxt = s_bufs[t % 2], s_bufs[(t + 1) % 2]
            if t == 0:
                cur[:, 0:tc] = scores(kc_ref[0:tc, :].astype(BF16))
            if t + 1 < n_cache:
                nxt[:, 0:tc] = scores(kc_ref[(t + 1) * tc:(t + 2) * tc, :].astype(BF16))
            elif t + 1 == n_cache:
                nxt[:, 0:tq] = scores(kb_ref[...])
            if t < n_cache:
                rows = pl.ds(t * tc, tc)
                carry = consume(carry, cur, tc, vc_ref[rows, :].astype(BF16), [f_ref[h, :, rows] for h in halves], None)
            else:
                carry = consume(carry, cur, tq, vb_ref[...], f_q_rows, row >= col)
    else:
        n_q = kn_ref.shape[0] // tq

        def tile_rows(t):
            return pl.ds(pl.multiple_of(jnp.minimum(t, n_q - 1) * tq, tq), tq)

        def consume_tile(carry, s_buf, t, masked):
            rows = tile_rows(t)
            visible = (t * tq + col) <= (q_start + row) if masked else None
            return consume(carry, s_buf, tq, vb_ref[rows, :], [f_ref[h, :, rows] for h in halves], visible)

        s_bufs[0][...] = scores(kb_ref[tile_rows(0), :])

        def pair_body(i, carry, masked):
            s_bufs[1][...] = scores(kb_ref[tile_rows(2 * i + 1), :])
            carry = consume_tile(carry, s_bufs[0], 2 * i, masked)
            s_bufs[0][...] = scores(kb_ref[tile_rows(2 * i + 2), :])
            return consume_tile(carry, s_bufs[1], 2 * i + 1, masked)

        full_pairs = qi // 2
        carry = lax.fori_loop(0, full_pairs, functools.partial(pair_body, masked=False), carry)
        carry = pair_body(full_pairs, carry, masked=True)

    out = carry[0][2] / carry[0][1]
    for h in list(halves)[1:]:
        out = jnp.where(in_half[h], carry[h][2] / carry[h][1], out)
    o_ref[...] = out.astype(o_ref.dtype)


def _fox_attention(q, k_new, v_new, f_rows, row0, n_batch, t_new, cache=None, *, heads, hd, tq, tc=512):
    width = heads * hd
    per_blk = LANES // hd
    assert t_new % tq == 0 and row0 % tq == 0
    assert tq % LANES == 0 or (tq == t_new and cache is not None)
    n_q = t_new // tq
    n_cache = 0
    in_specs = [pl.BlockSpec((tq, LANES), lambda b, h, i: (row0 // tq + b * n_q + i, h)),
                pl.BlockSpec((t_new, LANES), lambda b, h, i: (b, h)),
                pl.BlockSpec((t_new, LANES), lambda b, h, i: (b, h))]
    args = [q, k_new, v_new]
    if cache is not None:
        t_cache = cache[0].shape[1]
        tc = _tile(t_cache, tc, LANES)
        n_cache = t_cache // tc
        in_specs += [pl.BlockSpec((None, t_cache, LANES), lambda b, h, i: (b, 0, h)),
                     pl.BlockSpec((None, t_cache, LANES), lambda b, h, i: (b, 0, h))]
        args += list(cache)
    t_pad = f_rows.shape[-1]
    in_specs.append(pl.BlockSpec((None, per_blk, 1, t_pad), lambda b, h, i: (b, h, 0, 0)))
    args.append(f_rows)
    return pl.pallas_call(
        functools.partial(_fox_kernel, tq=tq, tc=tc, n_cache=n_cache, hd=hd),
        grid=(n_batch, width // LANES, n_q),
        in_specs=in_specs,
        out_specs=pl.BlockSpec((tq, LANES), lambda b, h, i: (b * n_q + i, h)),
        out_shape=jax.ShapeDtypeStruct((n_batch * t_new, width), BF16),
        scratch_shapes=[pltpu.VMEM((t_new, LANES), BF16), pltpu.VMEM((t_new, LANES), BF16)]
        + [pltpu.VMEM((per_blk * tq, max(tq, tc) if cache is not None else tq), F32)] * 2,
        compiler_params=_params(("arbitrary", "arbitrary", "arbitrary")),
        name="fox_cache" if cache is not None else "fox_prompt",
    )(*args)


def kernel(x_prompt, x_sample, c_prompt, c_sample, cache_k, cache_v, cache_logf, state_delta, state_conv, ada_w, ada_b, pre_mix_g, post_mix_g, pre_ffn_g, post_ffn_g, gdn_w_in, gdn_conv_w, gdn_a_log, gdn_dt_bias, gdn_o_gain, gdn_w_out, kv_norm_g, kv_ada_w, kv_ada_b, kv_w, kv_b_f, fox_w_q, fox_w_o, ffn_w_gate, ffn_w_up, ffn_w_down, moe_router_w, moe_router_b, moe_w_gate, moe_w_up, moe_w_down):
    bp, tp, d = x_prompt.shape
    bs, ts, _ = x_sample.shape
    n_p, n_s = bp * tp, bs * ts
    n = n_p + n_s
    depth = ada_w.shape[0]
    n_gdn = gdn_w_in.shape[0]
    gdn_heads, gdn_dk, gdn_dv = state_delta.shape[2:]
    qk_w, v_w = gdn_heads * gdn_dk, gdn_heads * gdn_dv
    conv_dim = 2 * qk_w + v_w
    fox_heads, fox_hd = cache_k.shape[2:]
    fox_w = fox_heads * fox_hd
    t_cache = cache_k.shape[1]
    n_experts = moe_router_w.shape[-1]
    assert tp % GROUP == 0 and ts % GROUP == 0

    x = jnp.concatenate([x_prompt.reshape(n_p, d), x_sample.reshape(n_s, d)], axis=0)
    c = jnp.concatenate([c_prompt, c_sample], axis=0)

    group_batch = np.concatenate([np.repeat(np.arange(bp), tp // GROUP),
                                  bp + np.repeat(np.arange(bs), ts // GROUP)])

    def per_group(m):
        n_l, _, width = m.shape
        rep = lambda a, b, t: jnp.broadcast_to(a[:, :, None, :], (n_l, b, t // GROUP, width)).reshape(n_l, -1, width)
        return jnp.concatenate([rep(m[:, :bp], bp, tp), rep(m[:, bp:], bs, ts)], axis=1)

    modg = per_group(_mod_matmul(c, ada_w, ada_b))
    kv_modg = per_group(_mod_matmul(c, kv_ada_w[None], kv_ada_b[None]))[0]

    seq_of_chunk = jnp.asarray(group_batch, jnp.int32)
    chunk_in_seq = np.concatenate([np.tile(np.arange(tp // GROUP), bp), np.tile(np.arange(ts // GROUP), bs)])
    seq_chunks = np.concatenate([np.full(n_p // GROUP, tp // GROUP), np.full(n_s // GROUP, ts // GROUP)])
    first = jnp.asarray(chunk_in_seq == 0, jnp.int32)
    last = jnp.asarray(chunk_in_seq == seq_chunks - 1, jnp.int32)

    cache_kv = tuple(a.reshape(bs, t_cache, fox_w).astype(BF16) for a in (cache_k, cache_v))
    delta_out, conv_out = [], []
    kv_p = kv_s = logf_p = logf_s = f_p = f_s = None
    for l in range(depth):
        g_mix = pre_mix_g[l]
        if l < n_gdn:
            in_w = gdn_w_in.shape[2]
            w_in = jnp.pad(gdn_w_in[l].astype(BF16), ((0, 0), (0, conv_dim + v_w + LANES - in_w)))
            proj = _norm_matmul(x, modg[l], 0, 1, g_mix, w_in, out_dtype=F32, tn=1408, name="gdn_in_proj")
            ctx = jnp.concatenate([jnp.zeros((bp, CONV_W - 1, conv_dim), F32), state_conv[:, l]], axis=0)
            ctx8 = jnp.pad(ctx, ((0, 0), (SUBLANES - (CONV_W - 1), 0), (0, 0)))
            s0 = jnp.concatenate([jnp.zeros((bp,) + state_delta.shape[2:], F32), state_delta[:, l]], axis=0)
            mixed, s_out, tail = _gdn_core(proj, ctx8, gdn_conv_w[l], gdn_a_log[l], gdn_dt_bias[l], gdn_o_gain[l],
                                     s0, seq_of_chunk, first, last, heads=gdn_heads, dk=gdn_dk, dv=gdn_dv)
            delta_out.append(s_out)
            conv_out.append(tail[:, SUBLANES - (CONV_W - 1):, :])
            w_o = gdn_w_out[l].astype(BF16)
        else:
            j = l - n_gdn
            q = _norm_matmul(x, modg[l], 0, 1, g_mix, fox_w_q[j].astype(BF16), out_dtype=BF16, name="fox_q_proj")
            o_p = _fox_attention(q, kv_p[0], kv_p[1], f_p, 0, bp, tp, heads=fox_heads, hd=fox_hd, tq=min(tp, 256))
            o_s = _fox_attention(q, kv_s[0], kv_s[1], f_s, n_p, bs, ts, cache=cache_kv, heads=fox_heads, hd=fox_hd, tq=ts)
            mixed = jnp.concatenate([o_p, o_s], axis=0)
            w_o = fox_w_o[j].astype(BF16)
        x = _matmul_post(mixed, w_o, x, modg[l], 2, post_mix_g[l], name="mixer_out_proj")

        i = l // 2
        if l % 2 == 0:
            x = _ffn(x, modg[l], pre_ffn_g[l], post_ffn_g[l], ffn_w_gate[i].astype(BF16),
                     ffn_w_up[i].astype(BF16), ffn_w_down[i].astype(BF16), name="dense_ffn")
        else:
            rw = jnp.zeros((d, LANES), F32).at[:, :n_experts].set(moe_router_w[i])
            rw_hi = rw.astype(BF16)
            rw_lo = (rw - rw_hi.astype(F32)).astype(BF16)
            rb = jnp.zeros((1, LANES), F32).at[0, :n_experts].set(moe_router_b[i])
            x = _moe(x, modg[l], pre_ffn_g[l], post_ffn_g[l], (rw_hi, rw_lo, rb, n_experts),
                     moe_w_gate[i].astype(BF16), moe_w_up[i].astype(BF16), moe_w_down[i].astype(BF16))

        if l == n_gdn - 1:
            w_k = kv_w[:, :fox_w].astype(BF16)
            w_v = kv_w[:, fox_w:2 * fox_w].astype(BF16)
            w_f = jnp.zeros((d, LANES), BF16).at[:, :fox_heads].set(kv_w[:, 2 * fox_w:].astype(BF16))
            b_f = jnp.zeros((LANES,), F32).at[:fox_heads].set(kv_b_f)
            *kv_p, lf_p = _kv_proj(x, kv_modg, kv_norm_g, w_k, w_v, w_f, b_f, (0, n_p))
            *kv_s, lf_s = _kv_proj(x, kv_modg, kv_norm_g, w_k, w_v, w_f, b_f, (n_p, n_s))
            logf_p = lf_p[:, :fox_heads].reshape(bp, tp, fox_heads)
            logf_s = lf_s[:, :fox_heads].reshape(bs, ts, fox_heads)
            f_p = _cumsum_rows(jnp.swapaxes(logf_p, 1, 2))[:, :, None, :]
            rows_s = jnp.swapaxes(jnp.concatenate([cache_logf, logf_s], axis=1), 1, 2)
            t_all = t_cache + ts
            t_pad = -(-t_all // LANES) * LANES
            f_s = _cumsum_rows(jnp.pad(rows_s, ((0, 0), (0, 0), (0, t_pad - t_all))))[:, :, None, :]

    delta = jnp.stack(delta_out, axis=1)
    conv = jnp.stack(conv_out, axis=1)
    return (x[:n_p].reshape(bp, tp, d), x[n_p:].reshape(bs, ts, d),
            kv_p[0].reshape(bp, tp, fox_heads, fox_hd), kv_p[1].reshape(bp, tp, fox_heads, fox_hd),
            logf_p, delta[:bp], conv[:bp],
            kv_s[0].reshape(bs, ts, fox_heads, fox_hd), kv_s[1].reshape(bs, ts, fox_heads, fox_hd),
            logf_s, delta[bp:], conv[bp:])
```
